```python
import math
import jax, jax.numpy as jnp
from jax import lax
import numpy as np

D_MODEL = 2048
BATCH = 2
SEQ = 4096
DEPTH = 4

N_BRANCH = 3
W_MIX = D_MODEL // 2
S5_GROUP = 16
S5_GROUPS = W_MIX // S5_GROUP
S5_STATE = 64
S5_DT_MIN = 1e-3
S5_DT_MAX = 1e-1
RG_BLOCKS = 16
RG_BLOCK = W_MIX // RG_BLOCKS
RG_CONV = 4
RG_C = 8.0
HG_HEADS = 8
HG_DK = W_MIX // HG_HEADS
HG_DV = W_MIX // HG_HEADS
HG_CHUNK = 64
EPS = 1e-6

N_IN = 8 * W_MIX + N_BRANCH * D_MODEL
SPLITS = tuple(W_MIX * k for k in range(1, 9))

kernel_name = "hybrid_s5_rglru_hgrn2_gated_merge"


def rms_norm(x, w):
    x32 = x.astype(jnp.float32)
    y = x32 * lax.rsqrt(jnp.mean(x32 * x32, axis=-1, keepdims=True) + EPS)
    return (y * w.astype(jnp.float32)).astype(x.dtype)


def s5_mixer(u, lam_re, lam_im, log_step, b_re, b_im, c_re, c_im, d, w_glu, b_glu):
    f32 = jnp.float32
    bsz, s, _ = u.shape
    u32 = u.astype(f32).reshape(bsz, s, S5_GROUPS, S5_GROUP)
    lam_re = lam_re.astype(f32)
    lam_im = lam_im.astype(f32)
    step = jnp.exp(log_step.astype(f32))[:, None]
    mag = jnp.exp(lam_re * step)
    ang = lam_im * step
    abar_re = mag * jnp.cos(ang)
    abar_im = mag * jnp.sin(ang)
    num_re = abar_re - 1.0
    num_im = abar_im
    den = lam_re * lam_re + lam_im * lam_im
    coef_re = (num_re * lam_re + num_im * lam_im) / den
    coef_im = (num_im * lam_re - num_re * lam_im) / den
    b_re = b_re.astype(f32)
    b_im = b_im.astype(f32)
    bbar_re = coef_re[..., None] * b_re - coef_im[..., None] * b_im
    bbar_im = coef_re[..., None] * b_im + coef_im[..., None] * b_re
    bu_re = jnp.einsum('bsgh,gph->bsgp', u32, bbar_re)
    bu_im = jnp.einsum('bsgh,gph->bsgp', u32, bbar_im)
    a_re = jnp.broadcast_to(abar_re, (1, s) + abar_re.shape)
    a_im = jnp.broadcast_to(abar_im, (1, s) + abar_im.shape)

    def combine(left, right):
        ar1, ai1, br1, bi1 = left
        ar2, ai2, br2, bi2 = right
        return (ar2 * ar1 - ai2 * ai1,
                ar2 * ai1 + ai2 * ar1,
                ar2 * br1 - ai2 * bi1 + br2,
                ar2 * bi1 + ai2 * br1 + bi2)

    _, _, x_re, x_im = lax.associative_scan(combine, (a_re, a_im, bu_re, bu_im), axis=1)
    y = (jnp.einsum('bsgp,ghp->bsgh', x_re, c_re.astype(f32))
         - jnp.einsum('bsgp,ghp->bsgh', x_im, c_im.astype(f32)))
    y = y.reshape(bsz, s, W_MIX) + d.astype(f32) * u32.reshape(bsz, s, W_MIX)
    y = jax.nn.gelu(y)
    y = y * jax.nn.sigmoid(y @ w_glu.astype(f32) + b_glu.astype(f32))
    return y.astype(u.dtype)


def rg_lru_mixer(x, conv_w, conv_b, w_a, b_a, w_x, b_x, lam):
    f32 = jnp.float32
    bsz, s, _ = x.shape
    x32 = x.astype(f32)
    xc = lax.conv_general_dilated(
        x32, conv_w.astype(f32).reshape(RG_CONV, 1, W_MIX),
        window_strides=(1,), padding=[(RG_CONV - 1, 0)],
        dimension_numbers=('NWC', 'WIO', 'NWC'),
        feature_group_count=W_MIX) + conv_b.astype(f32)
    xb = xc.reshape(bsz, s, RG_BLOCKS, RG_BLOCK)
    r = jax.nn.sigmoid(jnp.einsum('bsni,nij->bsnj', xb, w_a.astype(f32)).reshape(bsz, s, W_MIX)
                       + b_a.astype(f32))
    i = jax.nn.sigmoid(jnp.einsum('bsni,nij->bsnj', xb, w_x.astype(f32)).reshape(bsz, s, W_MIX)
                       + b_x.astype(f32))
    log_a = -RG_C * r * jax.nn.softplus(-lam.astype(f32))
    a = jnp.exp(log_a)
    mult = jnp.sqrt(-jnp.expm1(2.0 * log_a))
    is_first = (jnp.arange(s) == 0)[None, :, None]
    mult = jnp.where(is_first, jnp.ones_like(mult), mult)
    b_term = mult * (i * xc)

    def combine(left, right):
        a1, b1 = left
        a2, b2 = right
        return a2 * a1, a2 * b1 + b2

    _, h = lax.associative_scan(combine, (a, b_term), axis=1)
    return h.astype(x.dtype)


def hgrn2_mixer(q, f_logit, inp, lb, norm_w):
    f32 = jnp.float32
    bsz, s, _ = q.shape
    n_chunks = s // HG_CHUNK
    q32 = jax.nn.silu(q.astype(f32))
    f = lb + (1.0 - lb) * jax.nn.sigmoid(f_logit.astype(f32))
    k32 = 1.0 - f
    g32 = jnp.log(f)
    v32 = inp.astype(f32)

    def to_chunks(t, d):
        return t.reshape(bsz, n_chunks, HG_CHUNK, HG_HEADS, d).transpose(1, 0, 3, 2, 4)

    qc = to_chunks(q32, HG_DK)
    kc = to_chunks(k32, HG_DK)
    gc = to_chunks(g32, HG_DK)
    vc = to_chunks(v32, HG_DV)
    causal = jnp.tril(jnp.ones((HG_CHUNK, HG_CHUNK), dtype=bool))

    def chunk_step(state, xs):
        qx, kx, gx, vx = xs
        G = jnp.cumsum(gx, axis=2)
        inter = jnp.einsum('bhtk,bhkv->bhtv', qx * jnp.exp(G), state)
        diff = G[:, :, :, None, :] - G[:, :, None, :, :]
        decay = jnp.exp(jnp.where(causal[None, None, :, :, None], diff, -jnp.inf))
        attn = jnp.einsum('bhtk,bhsk,bhtsk->bhts', qx, kx, decay)
        intra = jnp.einsum('bhts,bhsv->bhtv', attn, vx)
        g_last = G[:, :, -1]
        k_dec = kx * jnp.exp(g_last[:, :, None, :] - G)
        new_state = (jnp.exp(g_last)[..., None] * state
                     + jnp.einsum('bhsk,bhsv->bhkv', k_dec, vx))
        return new_state, inter + intra

    state0 = jnp.zeros((bsz, HG_HEADS, HG_DK, HG_DV), f32)
    _, o = lax.scan(chunk_step, state0, (qc, kc, gc, vc))
    o = o.transpose(1, 0, 3, 2, 4).reshape(bsz, s, HG_HEADS, HG_DV)
    o = o * lax.rsqrt(jnp.mean(o * o, axis=-1, keepdims=True) + EPS)
    o = o * norm_w.astype(f32).reshape(HG_HEADS, HG_DV)
    return o.reshape(bsz, s, W_MIX).astype(q.dtype)


def setup_inputs(seed: int = 0) -> dict:
    key = jax.random.key(seed)
    ks = jax.random.split(key, 26)
    f32 = jnp.float32
    nrm = lambda k, shape, sc: sc * jax.random.normal(k, shape, f32)
    x = jax.random.normal(ks[0], (BATCH, SEQ, D_MODEL), f32)
    norm_w = 1.0 + nrm(ks[1], (DEPTH, D_MODEL), 0.02)
    w_in = nrm(ks[2], (DEPTH, D_MODEL, N_IN), D_MODEL ** -0.5)
    s5_lambda_re = -0.5 + nrm(ks[3], (DEPTH, S5_GROUPS, S5_STATE), 0.01)
    s5_lambda_im = (math.pi * jnp.arange(S5_STATE, dtype=f32)[None, None, :]
                    + nrm(ks[4], (DEPTH, S5_GROUPS, S5_STATE), 0.01))
    s5_log_step = jax.random.uniform(ks[5], (DEPTH, S5_GROUPS), f32,
                                     math.log(S5_DT_MIN), math.log(S5_DT_MAX))
    s5_b_re = nrm(ks[6], (DEPTH, S5_GROUPS, S5_STATE, S5_GROUP), (2 * S5_GROUP) ** -0.5)
    s5_b_im = nrm(ks[7], (DEPTH, S5_GROUPS, S5_STATE, S5_GROUP), (2 * S5_GROUP) ** -0.5)
    s5_c_re = nrm(ks[8], (DEPTH, S5_GROUPS, S5_GROUP, S5_STATE), (2 * S5_STATE) ** -0.5)
    s5_c_im = nrm(ks[9], (DEPTH, S5_GROUPS, S5_GROUP, S5_STATE), (2 * S5_STATE) ** -0.5)
    s5_d = nrm(ks[10], (DEPTH, W_MIX), 1.0)
    s5_w_glu = nrm(ks[11], (DEPTH, W_MIX, W_MIX), W_MIX ** -0.5)
    s5_b_glu = nrm(ks[12], (DEPTH, W_MIX), 0.01)
    rg_conv_w = nrm(ks[13], (DEPTH, RG_CONV, W_MIX), RG_CONV ** -0.5)
    rg_conv_b = nrm(ks[14], (DEPTH, W_MIX), 0.01)
    rg_w_a = nrm(ks[15], (DEPTH, RG_BLOCKS, RG_BLOCK, RG_BLOCK), RG_BLOCK ** -0.5)
    rg_b_a = nrm(ks[16], (DEPTH, W_MIX), 0.01)
    rg_w_x = nrm(ks[17], (DEPTH, RG_BLOCKS, RG_BLOCK, RG_BLOCK), RG_BLOCK ** -0.5)
    rg_b_x = nrm(ks[18], (DEPTH, W_MIX), 0.01)
    a0 = jax.random.uniform(ks[19], (DEPTH, W_MIX), f32, 0.9, 0.999)
    p = a0 ** (1.0 / RG_C)
    rg_lambda = jnp.log(p) - jnp.log1p(-p)
    hg_lower_bounds = nrm(ks[20], (DEPTH, W_MIX), 0.1)
    hg_norm_w = 1.0 + nrm(ks[21], (DEPTH, W_MIX), 0.02)
    w_branch = nrm(ks[22], (DEPTH, N_BRANCH, W_MIX, D_MODEL), W_MIX ** -0.5)
    w_out = nrm(ks[23], (DEPTH, D_MODEL, D_MODEL), D_MODEL ** -0.5)
    final_norm_w = 1.0 + nrm(ks[24], (D_MODEL,), 0.02)
    return {"x": x, "norm_w": norm_w, "w_in": w_in,
            "s5_lambda_re": s5_lambda_re, "s5_lambda_im": s5_lambda_im,
            "s5_log_step": s5_log_step, "s5_b_re": s5_b_re, "s5_b_im": s5_b_im,
            "s5_c_re": s5_c_re, "s5_c_im": s5_c_im, "s5_d": s5_d,
            "s5_w_glu": s5_w_glu, "s5_b_glu": s5_b_glu,
            "rg_conv_w": rg_conv_w, "rg_conv_b": rg_conv_b,
            "rg_w_a": rg_w_a, "rg_b_a": rg_b_a, "rg_w_x": rg_w_x, "rg_b_x": rg_b_x,
            "rg_lambda": rg_lambda,
            "hg_lower_bounds": hg_lower_bounds, "hg_norm_w": hg_norm_w,
            "w_branch": w_branch, "w_out": w_out, "final_norm_w": final_norm_w}


def reference(x, norm_w, w_in, s5_lambda_re, s5_lambda_im, s5_log_step, s5_b_re, s5_b_im,
              s5_c_re, s5_c_im, s5_d, s5_w_glu, s5_b_glu, rg_conv_w, rg_conv_b,
              rg_w_a, rg_b_a, rg_w_x, rg_b_x, rg_lambda, hg_lower_bounds, hg_norm_w,
              w_branch, w_out, final_norm_w):
    bsz, s, _ = x.shape
    lb_sm = jax.nn.softmax(hg_lower_bounds.astype(jnp.float32), axis=0)
    lbs = jnp.cumsum(lb_sm, axis=0) - lb_sm[0]
    for l in range(DEPTH):
        h = rms_norm(x, norm_w[l])
        z = h @ w_in[l]
        u_a, g_a, x_b, g_b, q_c, f_c, i_c, g_c, gate_logits = jnp.split(z, SPLITS, axis=-1)
        y_a = s5_mixer(u_a, s5_lambda_re[l], s5_lambda_im[l], s5_log_step[l],
                       s5_b_re[l], s5_b_im[l], s5_c_re[l], s5_c_im[l], s5_d[l],
                       s5_w_glu[l], s5_b_glu[l]) * jax.nn.silu(g_a)
        y_b = rg_lru_mixer(x_b, rg_conv_w[l], rg_conv_b[l], rg_w_a[l], rg_b_a[l],
                           rg_w_x[l], rg_b_x[l], rg_lambda[l]) * jax.nn.silu(g_b)
        y_c = hgrn2_mixer(q_c, f_c, i_c, lbs[l], hg_norm_w[l]) * jax.nn.silu(g_c)
        ys = jnp.stack([y_a, y_b, y_c], axis=2)
        branch = jnp.einsum('bsnw,nwd->bsnd', ys, w_branch[l])
        gates = jax.nn.sigmoid(gate_logits.reshape(bsz, s, N_BRANCH, D_MODEL))
        merged = jnp.sum(gates * branch, axis=2)
        x = x + merged @ w_out[l]
    return rms_norm(x, final_norm_w)
```

```python
import functools

import jax
import jax.numpy as jnp
from jax import lax
from jax.experimental import pallas as pl
from jax.experimental.pallas import tpu as pltpu

D_MODEL = 2048
DEPTH = 4
N_BRANCH = 3
W_MIX = D_MODEL // 2
S5_GROUP = 16
S5_GROUPS = W_MIX // S5_GROUP
S5_STATE = 64
RG_BLOCKS = 16
RG_BLOCK = W_MIX // RG_BLOCKS
RG_CONV = 4
RG_C = 8.0
HG_HEADS = 8
HG_DK = W_MIX // HG_HEADS
EPS = 1e-6
N_IN = 8 * W_MIX + N_BRANCH * D_MODEL

LANES = 128
SUBLANES = 8
VMEM_LIMIT = 56 * 1024 * 1024

S5_NJ = W_MIX // LANES
S5_HALF = (LANES // S5_GROUP) * S5_STATE
S5_KS_STEPS = 3

T_S5 = 256
T_RG = 256
T_HG = 128
RG_DIAG = 256
RG_ND = W_MIX // RG_DIAG

bf16 = jnp.bfloat16
f32 = jnp.float32


def _log2(n):
    assert n & (n - 1) == 0
    return n.bit_length() - 1


def _sigmoid(x):
    return 1.0 / (1.0 + jnp.exp(-x))


def _cparams(sem):
    return pltpu.CompilerParams(dimension_semantics=sem, vmem_limit_bytes=VMEM_LIMIT)


def _inproj_kernel(x_ref, nw_ref, w_ref, o_ref, h_ref):
    @pl.when(pl.program_id(1) == 0)
    def _():
        x = x_ref[...]
        ms = jnp.mean(x * x, axis=-1, keepdims=True)
        h_ref[...] = (x * lax.rsqrt(ms + EPS) * nw_ref[...]).astype(bf16)

    o_ref[...] = jnp.dot(h_ref[...], w_ref[...], preferred_element_type=f32)


def _inproj(x2, nw, w_bf, tm=512, tn=1024):
    m = x2.shape[0]
    return pl.pallas_call(
        _inproj_kernel,
        name="inproj",
        grid=(m // tm, N_IN // tn),
        in_specs=[
            pl.BlockSpec((tm, D_MODEL), lambda i, j: (i, 0)),
            pl.BlockSpec((1, D_MODEL), lambda i, j: (0, 0)),
            pl.BlockSpec((D_MODEL, tn), lambda i, j: (0, j)),
        ],
        out_specs=pl.BlockSpec((tm, tn), lambda i, j: (i, j)),
        out_shape=jax.ShapeDtypeStruct((m, N_IN), f32),
        scratch_shapes=[pltpu.VMEM((tm, D_MODEL), bf16)],
        compiler_params=_cparams(("parallel", "arbitrary")),
    )(x2, nw, w_bf)


def _s5_prep_kernel(lr_ref, li_ref, ls_ref, lrb_ref, lib_ref, lsb_ref, br_ref, bi_ref,
                    tre_ref, tim_ref, bbr_ref, bbi_ref):
    lr = lr_ref[0]
    li = li_ref[0]
    step = jnp.exp(ls_ref[0])
    shape = (SUBLANES, lr.shape[-1])
    row = lax.broadcasted_iota(jnp.int32, shape, 0)
    for k in range(S5_KS_STEPS + 1):
        if k < S5_KS_STEPS:
            n = jnp.full(shape, float(1 << k), f32)
            keep = row >= (1 << k)
        else:
            n = (row + 1).astype(f32)
            keep = row >= 0
        mag = jnp.exp(lr * step * n)
        ang = li * step * n
        tre_ref[0, k] = jnp.where(keep, mag * jnp.cos(ang), 0.0)
        tim_ref[0, k] = jnp.where(keep, mag * jnp.sin(ang), 0.0)
    lrb = lrb_ref[0]
    lib = lib_ref[0]
    stepb = jnp.exp(lsb_ref[0])
    mag = jnp.exp(lrb * stepb)
    ang = lib * stepb
    num_re = mag * jnp.cos(ang) - 1.0
    num_im = mag * jnp.sin(ang)
    den = lrb * lrb + lib * lib
    coef_re = (num_re * lrb + num_im * lib) / den
    coef_im = (num_im * lrb - num_re * lib) / den
    b_re = br_ref[0]
    b_im = bi_ref[0]
    bbr_ref[0] = coef_re * b_re - coef_im * b_im
    bbi_ref[0] = coef_re * b_im + coef_im * b_re


def _s5_prep(lam_re, lam_im, log_step, b_re, b_im):
    gp = S5_GROUPS * S5_STATE
    ph = S5_STATE * S5_GROUP
    flat = lambda a: a.reshape(DEPTH, 1, gp)
    ls_f = jnp.broadcast_to(log_step[:, :, None], (DEPTH, S5_GROUPS, S5_STATE))
    rep = lambda a: jnp.repeat(a, S5_GROUP, axis=2)
    ls_b = jnp.broadcast_to(log_step[:, :, None], (DEPTH, S5_GROUPS, ph))
    spec_f = pl.BlockSpec((1, 1, gp), lambda l: (l, 0, 0))
    spec_b = pl.BlockSpec((1, S5_GROUPS, ph), lambda l: (l, 0, 0))
    spec_t = pl.BlockSpec((1, S5_KS_STEPS + 1, SUBLANES, gp), lambda l: (l, 0, 0, 0))
    tre, tim, bbr, bbi = pl.pallas_call(
        _s5_prep_kernel,
        name="s5_prep",
        grid=(DEPTH,),
        in_specs=[spec_f, spec_f, spec_f, spec_b, spec_b, spec_b, spec_b, spec_b],
        out_specs=[spec_t, spec_t, spec_b, spec_b],
        out_shape=[
            jax.ShapeDtypeStruct((DEPTH, S5_KS_STEPS + 1, SUBLANES, gp), f32),
            jax.ShapeDtypeStruct((DEPTH, S5_KS_STEPS + 1, SUBLANES, gp), f32),
            jax.ShapeDtypeStruct((DEPTH, S5_GROUPS, ph), f32),
            jax.ShapeDtypeStruct((DEPTH, S5_GROUPS, ph), f32),
        ],
        compiler_params=_cparams(("parallel",)),
    )(flat(lam_re), flat(lam_im), flat(ls_f), rep(lam_re), rep(lam_im), ls_b,
      b_re.reshape(DEPTH, S5_GROUPS, ph), b_im.reshape(DEPTH, S5_GROUPS, ph))
    nt = S5_KS_STEPS + 1
    tab = jnp.stack([tre.reshape(DEPTH, nt, SUBLANES, S5_NJ, S5_HALF),
                     tim.reshape(DEPTH, nt, SUBLANES, S5_NJ, S5_HALF)], axis=4)
    tab = tab.reshape(DEPTH, nt, SUBLANES, S5_NJ * 2 * S5_HALF)
    gl = LANES // S5_GROUP
    eye = jnp.eye(gl, dtype=f32)
    bb = jnp.stack([bbr, bbi], axis=1).reshape(DEPTH, 2, S5_NJ, gl, S5_STATE, S5_GROUP)
    bbd = jnp.einsum('lcjgph,gk->ljghckp', bb, eye)
    bbd = bbd.reshape(DEPTH, S5_NJ, LANES, 2 * S5_HALF).astype(bf16)
    return tab, bbd


def _s5_c_layout(c):
    gl = LANES // S5_GROUP
    eye = jnp.eye(gl, dtype=f32)
    c6 = c.reshape(DEPTH, S5_NJ, gl, S5_GROUP, S5_STATE)
    cbd = jnp.einsum('ljghp,gk->ljgpkh', c6, eye)
    return cbd.reshape(DEPTH, S5_NJ, S5_HALF, LANES).astype(bf16)


def _lbs_kernel(x_ref, o_ref):
    rows = [x_ref[i:i + 1, :] for i in range(DEPTH)]
    m = functools.reduce(jnp.maximum, rows)
    es = [jnp.exp(r - m) for r in rows]
    tot = functools.reduce(lambda a, b: a + b, es)
    ps = [e / tot for e in es]
    acc = ps[0]
    out = [acc - ps[0]]
    for i in range(1, DEPTH):
        acc = acc + ps[i]
        out.append(acc - ps[0])
    o_ref[...] = jnp.concatenate(out, axis=0)


def _lbs(hg_lower_bounds):
    return pl.pallas_call(
        _lbs_kernel,
        name="hg_lower_bounds",
        out_shape=jax.ShapeDtypeStruct((DEPTH, W_MIX), f32),
    )(hg_lower_bounds)


def _s5_kernel(u_ref, ga_ref, bbd_ref, cre_ref, cim_ref, tab_ref, d_ref, wglu_ref, bglu_ref,
               o_ref, st_ref, x_ref, y_ref):
    t = u_ref.shape[0]
    hw = S5_HALF

    @pl.when(pl.program_id(1) == 0)
    def _():
        st_ref[...] = jnp.zeros_like(st_ref)

    for j in range(S5_NJ):
        lanes = slice(j * LANES, (j + 1) * LANES)
        tl = slice(j * 2 * hw, (j + 1) * 2 * hw)
        uj = u_ref[:, lanes]
        x_ref[...] = jnp.dot(uj.astype(bf16), bbd_ref[0, j], preferred_element_type=f32)

        def body(r, carry, tl=tl):
            row0 = pl.multiple_of(r * SUBLANES, SUBLANES)
            blk = x_ref[pl.ds(row0, SUBLANES), :]
            re = blk[:, :hw]
            im = blk[:, hw:]
            for k in range(S5_KS_STEPS):
                mk = tab_ref[0, k, :, tl]
                mre = mk[:, :hw]
                mim = mk[:, hw:]
                sre = pltpu.roll(re, 1 << k, 0)
                sim = pltpu.roll(im, 1 << k, 0)
                re, im = re + mre * sre - mim * sim, im + mre * sim + mim * sre
            ak = tab_ref[0, S5_KS_STEPS, :, tl]
            are = ak[:, :hw]
            aim = ak[:, hw:]
            cre = carry[:, :hw]
            cim = carry[:, hw:]
            re, im = re + are * cre - aim * cim, im + are * cim + aim * cre
            x_ref[pl.ds(row0, SUBLANES), :hw] = re
            x_ref[pl.ds(row0, SUBLANES), hw:] = im
            return jnp.concatenate([re[SUBLANES - 1:, :], im[SUBLANES - 1:, :]], axis=1)

        carry = lax.fori_loop(0, t // SUBLANES, body, st_ref[j:j + 1, :])
        st_ref[j:j + 1, :] = carry
        xs = x_ref[...]
        y = (jnp.dot(xs[:, :hw].astype(bf16), cre_ref[0, j], preferred_element_type=f32)
             - jnp.dot(xs[:, hw:].astype(bf16), cim_ref[0, j], preferred_element_type=f32))
        y = y + d_ref[0, :, lanes] * uj
        y_ref[:, lanes] = jax.nn.gelu(y)

    y = y_ref[...]
    gl = jnp.dot(y.astype(bf16), wglu_ref[0], preferred_element_type=f32) + bglu_ref[0]
    ga = ga_ref[...]
    o_ref[...] = y * _sigmoid(gl) * (ga * _sigmoid(ga))


def _s5(z, l, bsz, seq, bbd, cre, cim, tab, d, wglu, bglu):
    nc = seq // T_S5
    row = lambda b, c: b * nc + c
    lsel3 = lambda b, c: (l, 0, 0)
    lsel4 = lambda b, c: (l, 0, 0, 0)
    return pl.pallas_call(
        _s5_kernel,
        name="s5_mixer",
        grid=(bsz, nc),
        in_specs=[
            pl.BlockSpec((T_S5, W_MIX), lambda b, c: (row(b, c), 0)),
            pl.BlockSpec((T_S5, W_MIX), lambda b, c: (row(b, c), 1)),
            pl.BlockSpec((1, S5_NJ, LANES, 2 * S5_HALF), lsel4),
            pl.BlockSpec((1, S5_NJ, S5_HALF, LANES), lsel4),
            pl.BlockSpec((1, S5_NJ, S5_HALF, LANES), lsel4),
            pl.BlockSpec((1, S5_KS_STEPS + 1, SUBLANES, S5_NJ * 2 * S5_HALF), lsel4),
            pl.BlockSpec((1, 1, W_MIX), lsel3),
            pl.BlockSpec((1, W_MIX, W_MIX), lsel3),
            pl.BlockSpec((1, 1, W_MIX), lsel3),
        ],
        out_specs=pl.BlockSpec((T_S5, W_MIX), lambda b, c: (row(b, c), 0)),
        out_shape=jax.ShapeDtypeStruct((bsz * seq, W_MIX), f32),
        scratch_shapes=[
            pltpu.VMEM((S5_NJ, 2 * S5_HALF), f32),
            pltpu.VMEM((T_S5, 2 * S5_HALF), f32),
            pltpu.VMEM((T_S5, W_MIX), f32),
        ],
        compiler_params=_cparams(("parallel", "arbitrary")),
    )(z, z, bbd, cre, cim, tab, d, wglu, bglu)


def _rg_kernel(x_ref, g_ref, cw_ref, cb_ref, wa_ref, wx_ref, ba_ref, bx_ref, lam_ref,
               o_ref, tail_ref, h_ref, xe_ref, a_ref, b_ref):
    t = x_ref.shape[0]
    c = pl.program_id(1)

    @pl.when(c == 0)
    def _():
        tail_ref[...] = jnp.zeros_like(tail_ref)
        h_ref[...] = jnp.zeros_like(h_ref)

    xe_ref[0:SUBLANES, :] = tail_ref[...]
    xe_ref[SUBLANES:SUBLANES + t, :] = x_ref[...]
    tail_ref[...] = x_ref[t - SUBLANES:t, :]
    xc = cb_ref[0]
    for k in range(RG_CONV):
        off = SUBLANES - (RG_CONV - 1) + k
        xc = xc + cw_ref[0, k:k + 1, :] * xe_ref[off:off + t, :]

    rs, ins = [], []
    for m in range(RG_ND):
        sl = slice(m * RG_DIAG, (m + 1) * RG_DIAG)
        xs = xc[:, sl].astype(bf16)
        rs.append(jnp.dot(xs, wa_ref[0, m], preferred_element_type=f32))
        ins.append(jnp.dot(xs, wx_ref[0, m], preferred_element_type=f32))
    r = _sigmoid(jnp.concatenate(rs, axis=1) + ba_ref[0])
    i = _sigmoid(jnp.concatenate(ins, axis=1) + bx_ref[0])

    nlam = -lam_ref[0]
    softplus = jnp.maximum(nlam, 0.0) + jnp.log1p(jnp.exp(-jnp.abs(nlam)))
    log_a = (-RG_C * softplus) * r
    a = jnp.exp(log_a)
    mult = jnp.sqrt(-jnp.tanh(log_a) * (a * a + 1.0))
    row = lax.broadcasted_iota(jnp.int32, (t, W_MIX), 0)
    mult = jnp.where(jnp.logical_and(row == 0, c == 0), 1.0, mult)
    a_ref[...] = a
    b_ref[...] = mult * (i * xc)

    row8 = lax.broadcasted_iota(jnp.int32, (SUBLANES, W_MIX), 0)

    def body(rb, carry):
        row0 = pl.multiple_of(rb * SUBLANES, SUBLANES)
        av = a_ref[pl.ds(row0, SUBLANES), :]
        bv = b_ref[pl.ds(row0, SUBLANES), :]
        for k in range(3):
            s = 1 << k
            keep = row8 >= s
            sa = jnp.where(keep, pltpu.roll(av, s, 0), 1.0)
            sb = jnp.where(keep, pltpu.roll(bv, s, 0), 0.0)
            bv = bv + av * sb
            av = av * sa
        hv = bv + av * carry
        b_ref[pl.ds(row0, SUBLANES), :] = hv
        return hv[SUBLANES - 1:, :]

    carry = lax.fori_loop(0, t // SUBLANES, body, h_ref[...])
    h_ref[...] = carry
    g = g_ref[...]
    o_ref[...] = b_ref[...] * (g * _sigmoid(g))


def _rg(z, l, bsz, seq, cw, cb, wa_bd, wx_bd, ba, bx, lam):
    nc = seq // T_RG
    row = lambda b, c: b * nc + c
    lsel3 = lambda b, c: (l, 0, 0)
    lsel4 = lambda b, c: (l, 0, 0, 0)
    vec = pl.BlockSpec((1, 1, W_MIX), lsel3)
    return pl.pallas_call(
        _rg_kernel,
        name="rglru_mixer",
        grid=(bsz, nc),
        in_specs=[
            pl.BlockSpec((T_RG, W_MIX), lambda b, c: (row(b, c), 2)),
            pl.BlockSpec((T_RG, W_MIX), lambda b, c: (row(b, c), 3)),
            pl.BlockSpec((1, RG_CONV, W_MIX), lsel3),
            vec,
            pl.BlockSpec((1, RG_ND, RG_DIAG, RG_DIAG), lsel4),
            pl.BlockSpec((1, RG_ND, RG_DIAG, RG_DIAG), lsel4),
            vec, vec, vec,
        ],
        out_specs=pl.BlockSpec((T_RG, W_MIX), lambda b, c: (row(b, c), 0)),
        out_shape=jax.ShapeDtypeStruct((bsz * seq, W_MIX), f32),
        scratch_shapes=[
            pltpu.VMEM((SUBLANES, W_MIX), f32),
            pltpu.VMEM((1, W_MIX), f32),
            pltpu.VMEM((T_RG + SUBLANES, W_MIX), f32),
            pltpu.VMEM((T_RG, W_MIX), f32),
            pltpu.VMEM((T_RG, W_MIX), f32),
        ],
        compiler_params=_cparams(("parallel", "arbitrary")),
    )(z, z, cw, cb, wa_bd, wx_bd, ba, bx, lam)


def _rg_blockdiag(w):
    per = RG_DIAG // RG_BLOCK
    eye = jnp.eye(per, dtype=f32)
    w5 = w.reshape(DEPTH, RG_ND, per, RG_BLOCK, RG_BLOCK)
    bd = jnp.einsum('lmnij,nk->lmnikj', w5, eye)
    return bd.reshape(DEPTH, RG_ND, RG_DIAG, RG_DIAG).astype(bf16)


def _hg_kernel(q_ref, f_ref, i_ref, g_ref, lb_ref, nw_ref, o_ref, st_ref):
    t = q_ref.shape[0]
    dk = HG_DK

    @pl.when(pl.program_id(1) == 0)
    def _():
        st_ref[...] = jnp.zeros_like(st_ref)

    row = lax.broadcasted_iota(jnp.int32, (t, t), 0)
    col = lax.broadcasted_iota(jnp.int32, (t, t), 1)
    tri = (row >= col).astype(f32)
    rowk = lax.broadcasted_iota(jnp.int32, (t, dk), 0)
    rmod = rowk & (SUBLANES - 1)
    sel_r = lax.broadcasted_iota(jnp.int32, (SUBLANES * dk, t), 0) >> _log2(dk)
    sel_c = lax.broadcasted_iota(jnp.int32, (SUBLANES * dk, t), 1) & (SUBLANES - 1)
    esel = jnp.where(sel_r == sel_c, 1.0, 0.0).astype(bf16)
    nt_dims = (((1,), (1,)), ((), ()))
    tn_dims = (((0,), (0,)), ((), ()))

    for h in range(HG_HEADS):
        sl = slice(h * dk, (h + 1) * dk)
        qr = q_ref[:, sl]
        v = i_ref[:, sl]
        lb = lb_ref[0, :, sl]
        q = qr * _sigmoid(qr)
        fg = lb + (1.0 - lb) * _sigmoid(f_ref[:, sl])
        k = 1.0 - fg
        g = jnp.log(fg)
        gc = jnp.dot(tri, g, preferred_element_type=f32, precision=lax.Precision.HIGHEST)

        att = jnp.zeros((t, t), f32)
        m = t // 2
        while m >= SUBLANES:
            pieces = []
            for p in range(t // (2 * m)):
                rb = 2 * p * m + m - 1
                pieces.append(jnp.broadcast_to(gc[rb:rb + 1, :], (2 * m, dk)))
            bnd = pieces[0] if len(pieces) == 1 else jnp.concatenate(pieces, axis=0)
            upper = (rowk & (2 * m - 1)) >= m
            e = jnp.exp(jnp.where(upper, gc - bnd, bnd - gc))
            qt = jnp.where(upper, q * e, 0.0).astype(bf16)
            kt = jnp.where(upper, 0.0, k * e).astype(bf16)
            al = lax.dot_general(qt, kt, nt_dims, preferred_element_type=f32)
            if 2 * m < t:
                sh = _log2(2 * m)
                al = jnp.where((row >> sh) == (col >> sh), al, 0.0)
            att = att + al
            m //= 2

        g3 = gc.reshape(t // SUBLANES, SUBLANES, dk)
        k3 = k.reshape(t // SUBLANES, SUBLANES, dk)
        slabs = []
        for s in range(SUBLANES):
            gs = jnp.broadcast_to(g3[:, s:s + 1, :], g3.shape).reshape(t, dk)
            ks = jnp.broadcast_to(k3[:, s:s + 1, :], k3.shape).reshape(t, dk)
            keep = rmod >= s
            e = jnp.exp(jnp.where(keep, gc - gs, 0.0))
            slabs.append(jnp.where(keep, q * e * ks, 0.0).astype(bf16))
        pcat = jnp.concatenate(slabs, axis=1)
        ad = jnp.dot(pcat, esel, preferred_element_type=f32)
        sh = _log2(SUBLANES)
        att = att + jnp.where((row >> sh) == (col >> sh), ad, 0.0)

        st = st_ref[h]
        intra = jnp.dot(att.astype(bf16), v.astype(bf16), preferred_element_type=f32)
        qg = (q * jnp.exp(gc)).astype(bf16)
        inter = lax.dot_general(qg, st.astype(bf16), nt_dims, preferred_element_type=f32)
        glast = gc[t - 1:t, :]
        khat = (k * jnp.exp(glast - gc)).astype(bf16)
        st_ref[h] = jnp.exp(glast) * st + lax.dot_general(
            v.astype(bf16), khat, tn_dims, preferred_element_type=f32)

        o = intra + inter
        o = o * lax.rsqrt(jnp.mean(o * o, axis=-1, keepdims=True) + EPS) * nw_ref[0, :, sl]
        gate = g_ref[:, sl]
        o_ref[:, sl] = o * (gate * _sigmoid(gate))


def _hg(z, l, bsz, seq, lbs, nw):
    nc = seq // T_HG
    row = lambda b, c: b * nc + c
    lsel3 = lambda b, c: (l, 0, 0)
    zspec = lambda col: pl.BlockSpec((T_HG, W_MIX), lambda b, c: (row(b, c), col))
    vec = pl.BlockSpec((1, 1, W_MIX), lsel3)
    return pl.pallas_call(
        _hg_kernel,
        name="hgrn2_mixer",
        grid=(bsz, nc),
        in_specs=[zspec(4), zspec(5), zspec(6), zspec(7), vec, vec],
        out_specs=pl.BlockSpec((T_HG, W_MIX), lambda b, c: (row(b, c), 0)),
        out_shape=jax.ShapeDtypeStruct((bsz * seq, W_MIX), f32),
        scratch_shapes=[pltpu.VMEM((HG_HEADS, HG_DK, HG_DK), f32)],
        compiler_params=_cparams(("parallel", "arbitrary")),
    )(z, z, z, z, lbs, nw)


def _merge_kernel(ya_ref, yb_ref, yc_ref, g0_ref, g1_ref, g2_ref, x_ref, wb_ref, wo_ref, o_ref):
    merged = None
    for n, (y_ref, gl_ref) in enumerate(((ya_ref, g0_ref), (yb_ref, g1_ref), (yc_ref, g2_ref))):
        br = jnp.dot(y_ref[...].astype(bf16), wb_ref[0, n], preferred_element_type=f32)
        term = _sigmoid(gl_ref[...]) * br
        merged = term if merged is None else merged + term
    o_ref[...] = x_ref[...] + jnp.dot(merged.astype(bf16), wo_ref[0], preferred_element_type=f32)


def _merge(ya, yb, yc, z, x2, l, wb, wo, tm=256):
    m = x2.shape[0]
    yspec = pl.BlockSpec((tm, W_MIX), lambda i: (i, 0))
    gspec = lambda n: pl.BlockSpec((tm, D_MODEL), lambda i: (i, 4 + n))
    return pl.pallas_call(
        _merge_kernel,
        name="merge_out",
        grid=(m // tm,),
        in_specs=[
            yspec, yspec, yspec, gspec(0), gspec(1), gspec(2),
            pl.BlockSpec((tm, D_MODEL), lambda i: (i, 0)),
            pl.BlockSpec((1, N_BRANCH, W_MIX, D_MODEL), lambda i: (l, 0, 0, 0),
                         pipeline_mode=pl.Buffered(1)),
            pl.BlockSpec((1, D_MODEL, D_MODEL), lambda i: (l, 0, 0),
                         pipeline_mode=pl.Buffered(1)),
        ],
        out_specs=pl.BlockSpec((tm, D_MODEL), lambda i: (i, 0)),
        out_shape=jax.ShapeDtypeStruct((m, D_MODEL), f32),
        compiler_params=_cparams(("parallel",)),
    )(ya, yb, yc, z, z, z, x2, wb, wo)


def _final_norm_kernel(x_ref, w_ref, o_ref):
    x = x_ref[...]
    ms = jnp.mean(x * x, axis=-1, keepdims=True)
    o_ref[...] = x * lax.rsqrt(ms + EPS) * w_ref[...]


def _final_norm(x2, w, tm=512):
    m = x2.shape[0]
    return pl.pallas_call(
        _final_norm_kernel,
        name="final_norm",
        grid=(m // tm,),
        in_specs=[pl.BlockSpec((tm, D_MODEL), lambda i: (i, 0)),
                  pl.BlockSpec((1, D_MODEL), lambda i: (0, 0))],
        out_specs=pl.BlockSpec((tm, D_MODEL), lambda i: (i, 0)),
        out_shape=jax.ShapeDtypeStruct((m, D_MODEL), f32),
        compiler_params=_cparams(("parallel",)),
    )(x2, w)


def kernel(x, norm_w, w_in, s5_lambda_re, s5_lambda_im, s5_log_step, s5_b_re, s5_b_im, s5_c_re, s5_c_im, s5_d, s5_w_glu, s5_b_glu, rg_conv_w, rg_conv_b, rg_w_a, rg_b_a, rg_w_x, rg_b_x, rg_lambda, hg_lower_bounds, hg_norm_w, w_branch, w_out, final_norm_w):
    bsz, seq, _ = x.shape
    assert seq % T_S5 == 0 and seq % T_RG == 0 and seq % T_HG == 0
    x2 = x.reshape(bsz * seq, D_MODEL)

    w_in_bf = w_in.astype(bf16)
    tab, bbd = _s5_prep(s5_lambda_re, s5_lambda_im, s5_log_step, s5_b_re, s5_b_im)
    cre = _s5_c_layout(s5_c_re)
    cim = _s5_c_layout(s5_c_im)
    wglu_bf = s5_w_glu.astype(bf16)
    wa_bd = _rg_blockdiag(rg_w_a)
    wx_bd = _rg_blockdiag(rg_w_x)
    lbs = _lbs(hg_lower_bounds.astype(f32))
    wb_bf = w_branch.astype(bf16)
    wo_bf = w_out.astype(bf16)
    vec3 = lambda a: a.reshape(DEPTH, 1, -1)

    for l in range(DEPTH):
        z = _inproj(x2, norm_w[l].reshape(1, D_MODEL), w_in_bf[l])
        ya = _s5(z, l, bsz, seq, bbd, cre, cim, tab, vec3(s5_d), wglu_bf, vec3(s5_b_glu))
        yb = _rg(z, l, bsz, seq, rg_conv_w, vec3(rg_conv_b), wa_bd, wx_bd,
                 vec3(rg_b_a), vec3(rg_b_x), vec3(rg_lambda))
        yc = _hg(z, l, bsz, seq, vec3(lbs), vec3(hg_norm_w))
        x2 = _merge(ya, yb, yc, z, x2, l, wb_bf, wo_bf)

    out = _final_norm(x2, final_norm_w.reshape(1, D_MODEL))
    return out.reshape(bsz, seq, D_MODEL)
```

```python
import functools

import numpy as np
import jax
import jax.numpy as jnp
from jax import lax
from jax.experimental import pallas as pl
from jax.experimental.pallas import tpu as pltpu

D_MODEL = 2048
DEPTH = 4
N_BRANCH = 3
W_MIX = D_MODEL // 2
S5_GROUP = 16
S5_GROUPS = W_MIX // S5_GROUP
S5_STATE = 64
RG_BLOCKS = 16
RG_BLOCK = W_MIX // RG_BLOCKS
RG_CONV = 4
RG_C = 8.0
HG_HEADS = 8
HG_DK = W_MIX // HG_HEADS
EPS = 1e-6
N_IN = 8 * W_MIX + N_BRANCH * D_MODEL

LANES = 128
SUBLANES = 8
BF16_ROWS = 16
VMEM_LIMIT = 56 * 1024 * 1024

S5_NJ = W_MIX // LANES
S5_HALF = (LANES // S5_GROUP) * S5_STATE
S5_KS_STEPS = 3
S5_NTAB = S5_KS_STEPS + 2

T_S5 = 256
S5_SEG = T_S5 // SUBLANES
T_RG = 256
T_HG = 128
HG_LEVELS = tuple(T_HG >> (i + 1) for i in range(T_HG.bit_length() - 1))
RG_DIAG = 256
RG_ND = W_MIX // RG_DIAG

bf16 = jnp.bfloat16
f32 = jnp.float32


def _log2(n):
    assert n & (n - 1) == 0
    return n.bit_length() - 1


def _sigmoid(x):
    return 1.0 / (1.0 + jnp.exp(-x))


def _cparams(sem):
    return pltpu.CompilerParams(dimension_semantics=sem, vmem_limit_bytes=VMEM_LIMIT)


def _rms(x, w):
    ms = jnp.mean(x * x, axis=-1, keepdims=True)
    return x * lax.rsqrt(ms + EPS) * w


def _norm_kernel(x_ref, w_ref, o_ref):
    o_ref[...] = _rms(x_ref[...], w_ref[...]).astype(bf16)


def _norm_bf16(x2, w, tm=512):
    m = x2.shape[0]
    return pl.pallas_call(
        _norm_kernel,
        name="first_norm",
        grid=(m // tm,),
        in_specs=[pl.BlockSpec((tm, D_MODEL), lambda i: (i, 0)),
                  pl.BlockSpec((1, D_MODEL), lambda i: (0, 0))],
        out_specs=pl.BlockSpec((tm, D_MODEL), lambda i: (i, 0)),
        out_shape=jax.ShapeDtypeStruct((m, D_MODEL), bf16),
        compiler_params=_cparams(("parallel",)),
    )(x2, w)


def _inproj_kernel(h_ref, w_ref, o_ref, wbf_ref):
    @pl.when(pl.program_id(1) == 0)
    def _():
        wbf_ref[...] = w_ref[0].astype(bf16)

    o_ref[...] = jnp.dot(h_ref[...], wbf_ref[...], preferred_element_type=f32).astype(bf16)


def _inproj(h, w_in, l, tm=1024, tn=1024):
    m = h.shape[0]
    return pl.pallas_call(
        _inproj_kernel,
        name="inproj",
        grid=(N_IN // tn, m // tm),
        in_specs=[
            pl.BlockSpec((tm, D_MODEL), lambda j, i: (i, 0)),
            pl.BlockSpec((1, D_MODEL, tn), lambda j, i: (l, 0, j)),
        ],
        out_specs=pl.BlockSpec((tm, tn), lambda j, i: (i, j)),
        out_shape=jax.ShapeDtypeStruct((m, N_IN), bf16),
        scratch_shapes=[pltpu.VMEM((D_MODEL, tn), bf16)],
        compiler_params=_cparams(("arbitrary", "arbitrary")),
    )(h, w_in)


def _s5_prep_kernel(lr_ref, li_ref, ls_ref, lrb_ref, lib_ref, lsb_ref, br_ref, bi_ref,
                    tre_ref, tim_ref, bbr_ref, bbi_ref):
    lr = lr_ref[0]
    li = li_ref[0]
    step = jnp.exp(ls_ref[0])
    gp = lr.shape[-1]

    def power(n):
        mag = jnp.exp(lr * step * n)
        ang = li * step * n
        return mag * jnp.cos(ang), mag * jnp.sin(ang)

    row = lax.broadcasted_iota(jnp.int32, (SUBLANES, gp), 0)
    for k in range(S5_NTAB):
        keep = row >= 0
        if k < S5_KS_STEPS:
            n = jnp.full((SUBLANES, gp), float(S5_SEG << k), f32)
            keep = row >= (1 << k)
        elif k == S5_KS_STEPS:
            n = ((row + 1) * S5_SEG).astype(f32)
        else:
            n = jnp.full((SUBLANES, gp), 1.0, f32)
        pr, pi = power(n)
        tre_ref[0, k] = jnp.where(keep, pr, 0.0)
        tim_ref[0, k] = jnp.where(keep, pi, 0.0)
    lrb = lrb_ref[0]
    lib = lib_ref[0]
    stepb = jnp.exp(lsb_ref[0])
    mag = jnp.exp(lrb * stepb)
    ang = lib * stepb
    num_re = mag * jnp.cos(ang) - 1.0
    num_im = mag * jnp.sin(ang)
    den = lrb * lrb + lib * lib
    coef_re = (num_re * lrb + num_im * lib) / den
    coef_im = (num_im * lrb - num_re * lib) / den
    b_re = br_ref[0]
    b_im = bi_ref[0]
    bbr_ref[0] = coef_re * b_re - coef_im * b_im
    bbi_ref[0] = coef_re * b_im + coef_im * b_re


def _s5_prep(lam_re, lam_im, log_step, b_re, b_im):
    gp = S5_GROUPS * S5_STATE
    ph = S5_STATE * S5_GROUP
    nt = S5_NTAB
    flat = lambda a: a.reshape(DEPTH, 1, gp)
    ls_f = jnp.broadcast_to(log_step[:, :, None], (DEPTH, S5_GROUPS, S5_STATE))
    rep = lambda a: jnp.repeat(a, S5_GROUP, axis=2)
    ls_b = jnp.broadcast_to(log_step[:, :, None], (DEPTH, S5_GROUPS, ph))
    spec_f = pl.BlockSpec((1, 1, gp), lambda l: (l, 0, 0))
    spec_b = pl.BlockSpec((1, S5_GROUPS, ph), lambda l: (l, 0, 0))
    spec_t = pl.BlockSpec((1, nt, SUBLANES, gp), lambda l: (l, 0, 0, 0))
    tre, tim, bbr, bbi = pl.pallas_call(
        _s5_prep_kernel,
        name="s5_prep",
        grid=(DEPTH,),
        in_specs=[spec_f, spec_f, spec_f, spec_b, spec_b, spec_b, spec_b, spec_b],
        out_specs=[spec_t, spec_t, spec_b, spec_b],
        out_shape=[
            jax.ShapeDtypeStruct((DEPTH, nt, SUBLANES, gp), f32),
            jax.ShapeDtypeStruct((DEPTH, nt, SUBLANES, gp), f32),
            jax.ShapeDtypeStruct((DEPTH, S5_GROUPS, ph), f32),
            jax.ShapeDtypeStruct((DEPTH, S5_GROUPS, ph), f32),
        ],
        compiler_params=_cparams(("parallel",)),
    )(flat(lam_re), flat(lam_im), flat(ls_f), rep(lam_re), rep(lam_im), ls_b,
      b_re.reshape(DEPTH, S5_GROUPS, ph), b_im.reshape(DEPTH, S5_GROUPS, ph))

    def interleave(re, im):
        lead = re.shape[:-1]
        both = jnp.stack([re.reshape(*lead, S5_NJ, S5_HALF), im.reshape(*lead, S5_NJ, S5_HALF)],
                         axis=-2)
        return both.reshape(*lead, S5_NJ * 2 * S5_HALF)

    tab = interleave(tre, tim)
    gl = LANES // S5_GROUP
    eye = jnp.eye(gl, dtype=f32)
    bb = jnp.stack([bbr, bbi], axis=1).reshape(DEPTH, 2, S5_NJ, gl, S5_STATE, S5_GROUP)
    bbd = jnp.einsum('lcjgph,gk->ljghckp', bb, eye)
    bbd = bbd.reshape(DEPTH, S5_NJ, LANES, 2 * S5_HALF).astype(bf16)
    return tab, bbd


def _s5_c_layout(c):
    gl = LANES // S5_GROUP
    eye = jnp.eye(gl, dtype=f32)
    c6 = c.reshape(DEPTH, S5_NJ, gl, S5_GROUP, S5_STATE)
    cbd = jnp.einsum('ljghp,gk->ljgpkh', c6, eye)
    return cbd.reshape(DEPTH, S5_NJ, S5_HALF, LANES).astype(bf16)


def _s5_segment_perm():
    pm = np.zeros((T_S5, T_S5), np.float32)
    for r in range(SUBLANES):
        for t in range(S5_SEG):
            pm[t * SUBLANES + r, r * S5_SEG + t] = 1.0
    return jnp.asarray(pm, bf16), jnp.asarray(pm.T, bf16)


def _lbs_kernel(x_ref, o_ref):
    rows = [x_ref[i:i + 1, :] for i in range(DEPTH)]
    m = functools.reduce(jnp.maximum, rows)
    es = [jnp.exp(r - m) for r in rows]
    tot = functools.reduce(lambda a, b: a + b, es)
    ps = [e / tot for e in es]
    acc = ps[0]
    out = [acc - ps[0]]
    for i in range(1, DEPTH):
        acc = acc + ps[i]
        out.append(acc - ps[0])
    o_ref[...] = jnp.concatenate(out, axis=0)


def _lbs(hg_lower_bounds):
    return pl.pallas_call(
        _lbs_kernel,
        name="hg_lower_bounds",
        out_shape=jax.ShapeDtypeStruct((DEPTH, W_MIX), f32),
    )(hg_lower_bounds)


def _s5_kernel(u_ref, ga_ref, pm_ref, pmt_ref, bbd_ref, cre_ref, cim_ref, tab_ref,
               d_ref, wglu_ref, bglu_ref, o_ref, st_ref, bu_ref, xb_ref, y_ref):
    hw = S5_HALF
    r8 = SUBLANES

    @pl.when(pl.program_id(1) == 0)
    def _():
        st_ref[...] = jnp.zeros_like(st_ref)

    pm = pm_ref[...]
    up = jnp.dot(pm, u_ref[...], preferred_element_type=f32)
    gap = jnp.dot(pm, ga_ref[...], preferred_element_type=f32)
    up_bf = up.astype(bf16)
    row8 = lax.broadcasted_iota(jnp.int32, (r8, hw), 0)

    for j in range(S5_NJ):
        slot = j % 2
        lanes = slice(j * LANES, (j + 1) * LANES)
        tl = slice(j * 2 * hw, (j + 1) * 2 * hw)
        bu_ref[slot] = jnp.dot(up_bf[:, lanes], bbd_ref[0, j], preferred_element_type=f32)
        a1 = tab_ref[0, S5_KS_STEPS + 1, :, tl]
        are = a1[:, :hw]
        aim = a1[:, hw:]

        xre = bu_ref[slot, 0:r8, :hw]
        xim = bu_ref[slot, 0:r8, hw:]
        for t in range(1, S5_SEG):
            rows = slice(t * r8, (t + 1) * r8)
            xre, xim = (are * xre - aim * xim + bu_ref[slot, rows, :hw],
                        are * xim + aim * xre + bu_ref[slot, rows, hw:])
            bu_ref[slot, rows, :hw] = xre
            bu_ref[slot, rows, hw:] = xim

        for k in range(S5_KS_STEPS):
            mk = tab_ref[0, k, :, tl]
            mre = mk[:, :hw]
            mim = mk[:, hw:]
            sre = pltpu.roll(xre, 1 << k, 0)
            sim = pltpu.roll(xim, 1 << k, 0)
            xre, xim = xre + mre * sre - mim * sim, xim + mre * sim + mim * sre
        ak = tab_ref[0, S5_KS_STEPS, :, tl]
        cre = st_ref[j:j + 1, :hw]
        cim = st_ref[j:j + 1, hw:]
        fre = xre + ak[:, :hw] * cre - ak[:, hw:] * cim
        fim = xim + ak[:, :hw] * cim + ak[:, hw:] * cre
        st_ref[j:j + 1, :hw] = fre[r8 - 1:, :]
        st_ref[j:j + 1, hw:] = fim[r8 - 1:, :]
        ire = jnp.where(row8 == 0, cre, pltpu.roll(fre, 1, 0))
        iim = jnp.where(row8 == 0, cim, pltpu.roll(fim, 1, 0))

        wre, wim = ire, iim
        for t2 in range(S5_SEG // 2):
            rows = slice(t2 * BF16_ROWS, (t2 + 1) * BF16_ROWS)
            wre0, wim0 = are * wre - aim * wim, are * wim + aim * wre
            wre, wim = are * wre0 - aim * wim0, are * wim0 + aim * wre0
            xr = bu_ref[slot, rows, :hw] + jnp.concatenate([wre0, wre], axis=0)
            xi = bu_ref[slot, rows, hw:] + jnp.concatenate([wim0, wim], axis=0)
            xb_ref[slot, rows, :hw] = xr.astype(bf16)
            xb_ref[slot, rows, hw:] = xi.astype(bf16)

        y = (jnp.dot(xb_ref[slot, :, :hw], cre_ref[0, j], preferred_element_type=f32)
             - jnp.dot(xb_ref[slot, :, hw:], cim_ref[0, j], preferred_element_type=f32))
        y = y + d_ref[0, :, lanes] * up[:, lanes]
        y_ref[:, lanes] = jax.nn.gelu(y)

    y = y_ref[...]
    gl = jnp.dot(y.astype(bf16), wglu_ref[0], preferred_element_type=f32) + bglu_ref[0]
    outp = (y * _sigmoid(gl) * (gap * _sigmoid(gap))).astype(bf16)
    o_ref[...] = jnp.dot(pmt_ref[...], outp, preferred_element_type=f32).astype(bf16)


def _s5(z, l, bsz, seq, pm, pmt, bbd, cre, cim, tab, d, wglu, bglu):
    nc = seq // T_S5
    row = lambda b, c: b * nc + c
    lsel3 = lambda b, c: (l, 0, 0)
    lsel4 = lambda b, c: (l, 0, 0, 0)
    perm = pl.BlockSpec((T_S5, T_S5), lambda b, c: (0, 0))
    return pl.pallas_call(
        _s5_kernel,
        name="s5_mixer",
        grid=(bsz, nc),
        in_specs=[
            pl.BlockSpec((T_S5, W_MIX), lambda b, c: (row(b, c), 0)),
            pl.BlockSpec((T_S5, W_MIX), lambda b, c: (row(b, c), 1)),
            perm, perm,
            pl.BlockSpec((1, S5_NJ, LANES, 2 * S5_HALF), lsel4),
            pl.BlockSpec((1, S5_NJ, S5_HALF, LANES), lsel4),
            pl.BlockSpec((1, S5_NJ, S5_HALF, LANES), lsel4),
            pl.BlockSpec((1, S5_NTAB, SUBLANES, S5_NJ * 2 * S5_HALF), lsel4),
            pl.BlockSpec((1, 1, W_MIX), lsel3),
            pl.BlockSpec((1, W_MIX, W_MIX), lsel3),
            pl.BlockSpec((1, 1, W_MIX), lsel3),
        ],
        out_specs=pl.BlockSpec((T_S5, W_MIX), lambda b, c: (row(b, c), 0)),
        out_shape=jax.ShapeDtypeStruct((bsz * seq, W_MIX), bf16),
        scratch_shapes=[
            pltpu.VMEM((S5_NJ, 2 * S5_HALF), f32),
            pltpu.VMEM((2, T_S5, 2 * S5_HALF), f32),
            pltpu.VMEM((2, T_S5, 2 * S5_HALF), bf16),
            pltpu.VMEM((T_S5, W_MIX), f32),
        ],
        compiler_params=_cparams(("parallel", "arbitrary")),
    )(z, z, pm, pmt, bbd, cre, cim, tab, d, wglu, bglu)


def _rg_kernel(x_ref, g_ref, cw_ref, cb_ref, wa_ref, wx_ref, ba_ref, bx_ref, lam_ref,
               o_ref, tail_ref, h_ref, xe_ref, a_ref, b_ref):
    t = x_ref.shape[0]
    c = pl.program_id(1)

    @pl.when(c == 0)
    def _():
        tail_ref[...] = jnp.zeros_like(tail_ref)
        h_ref[...] = jnp.zeros_like(h_ref)

    x = x_ref[...].astype(f32)
    xe_ref[0:SUBLANES, :] = tail_ref[...]
    xe_ref[SUBLANES:SUBLANES + t, :] = x
    tail_ref[...] = x[t - SUBLANES:t, :]
    xc = cb_ref[0]
    for k in range(RG_CONV):
        off = SUBLANES - (RG_CONV - 1) + k
        xc = xc + cw_ref[0, k:k + 1, :] * xe_ref[off:off + t, :]

    rs, ins = [], []
    for m in range(RG_ND):
        sl = slice(m * RG_DIAG, (m + 1) * RG_DIAG)
        xs = xc[:, sl].astype(bf16)
        rs.append(jnp.dot(xs, wa_ref[0, m], preferred_element_type=f32))
        ins.append(jnp.dot(xs, wx_ref[0, m], preferred_element_type=f32))
    r = _sigmoid(jnp.concatenate(rs, axis=1) + ba_ref[0])
    i = _sigmoid(jnp.concatenate(ins, axis=1) + bx_ref[0])

    nlam = -lam_ref[0]
    softplus = jnp.maximum(nlam, 0.0) + jnp.log1p(jnp.exp(-jnp.abs(nlam)))
    log_a = (-RG_C * softplus) * r
    a = jnp.exp(log_a)
    mult = jnp.sqrt(-jnp.tanh(log_a) * (a * a + 1.0))
    row = lax.broadcasted_iota(jnp.int32, (t, W_MIX), 0)
    mult = jnp.where(jnp.logical_and(row == 0, c == 0), 1.0, mult)
    a_ref[...] = a
    b_ref[...] = mult * (i * xc)

    row8 = lax.broadcasted_iota(jnp.int32, (SUBLANES, W_MIX), 0)

    def body(rb, carry):
        row0 = pl.multiple_of(rb * SUBLANES, SUBLANES)
        av = a_ref[pl.ds(row0, SUBLANES), :]
        bv = b_ref[pl.ds(row0, SUBLANES), :]
        for k in range(3):
            s = 1 << k
            keep = row8 >= s
            sa = jnp.where(keep, pltpu.roll(av, s, 0), 1.0)
            sb = jnp.where(keep, pltpu.roll(bv, s, 0), 0.0)
            bv = bv + av * sb
            av = av * sa
        hv = bv + av * carry
        b_ref[pl.ds(row0, SUBLANES), :] = hv
        return hv[SUBLANES - 1:, :]

    carry = lax.fori_loop(0, t // SUBLANES, body, h_ref[...])
    h_ref[...] = carry
    g = g_ref[...].astype(f32)
    o_ref[...] = (b_ref[...] * (g * _sigmoid(g))).astype(bf16)


def _rg(z, l, bsz, seq, cw, cb, wa_bd, wx_bd, ba, bx, lam):
    nc = seq // T_RG
    row = lambda b, c: b * nc + c
    lsel3 = lambda b, c: (l, 0, 0)
    lsel4 = lambda b, c: (l, 0, 0, 0)
    vec = pl.BlockSpec((1, 1, W_MIX), lsel3)
    return pl.pallas_call(
        _rg_kernel,
        name="rglru_mixer",
        grid=(bsz, nc),
        in_specs=[
            pl.BlockSpec((T_RG, W_MIX), lambda b, c: (row(b, c), 2)),
            pl.BlockSpec((T_RG, W_MIX), lambda b, c: (row(b, c), 3)),
            pl.BlockSpec((1, RG_CONV, W_MIX), lsel3),
            vec,
            pl.BlockSpec((1, RG_ND, RG_DIAG, RG_DIAG), lsel4),
            pl.BlockSpec((1, RG_ND, RG_DIAG, RG_DIAG), lsel4),
            vec, vec, vec,
        ],
        out_specs=pl.BlockSpec((T_RG, W_MIX), lambda b, c: (row(b, c), 0)),
        out_shape=jax.ShapeDtypeStruct((bsz * seq, W_MIX), bf16),
        scratch_shapes=[
            pltpu.VMEM((SUBLANES, W_MIX), f32),
            pltpu.VMEM((1, W_MIX), f32),
            pltpu.VMEM((T_RG + SUBLANES, W_MIX), f32),
            pltpu.VMEM((T_RG, W_MIX), f32),
            pltpu.VMEM((T_RG, W_MIX), f32),
        ],
        compiler_params=_cparams(("parallel", "arbitrary")),
    )(z, z, cw, cb, wa_bd, wx_bd, ba, bx, lam)


def _rg_blockdiag(w):
    per = RG_DIAG // RG_BLOCK
    eye = jnp.eye(per, dtype=f32)
    w5 = w.reshape(DEPTH, RG_ND, per, RG_BLOCK, RG_BLOCK)
    bd = jnp.einsum('lmnij,nk->lmnikj', w5, eye)
    return bd.reshape(DEPTH, RG_ND, RG_DIAG, RG_DIAG).astype(bf16)


def _hg_level_masks():
    t = np.arange(T_HG)[:, None]
    s = np.arange(T_HG)[None, :]
    masks = [((t // (2 * m)) == (s // (2 * m))) & ((t & m) != 0) & ((s & m) == 0)
             for m in HG_LEVELS]
    return jnp.asarray(np.stack(masks).astype(np.float32))


def _hg_kernel(q_ref, f_ref, i_ref, g_ref, lm_ref, lb_ref, nw_ref, o_ref, st_ref):
    t = q_ref.shape[0]
    dk = HG_DK
    nb = t // SUBLANES

    @pl.when(pl.program_id(1) == 0)
    def _():
        st_ref[...] = jnp.zeros_like(st_ref)

    row = lax.broadcasted_iota(jnp.int32, (t, t), 0)
    col = lax.broadcasted_iota(jnp.int32, (t, t), 1)
    tri = (row >= col).astype(f32)
    rowk = lax.broadcasted_iota(jnp.int32, (t, dk), 0)
    nt_dims = (((1,), (1,)), ((), ()))
    tn_dims = (((0,), (0,)), ((), ()))

    for h in range(HG_HEADS):
        sl = slice(h * dk, (h + 1) * dk)
        qr = q_ref[:, sl].astype(f32)
        v = i_ref[:, sl]
        lb = lb_ref[0, :, sl]
        q = qr * _sigmoid(qr)
        fg = lb + (1.0 - lb) * _sigmoid(f_ref[:, sl].astype(f32))
        k = 1.0 - fg
        g = jnp.log(fg)
        gc = jnp.dot(tri, g, preferred_element_type=f32, precision=lax.Precision.HIGHEST)
        g3 = gc.reshape(nb, SUBLANES, dk)

        def bcast_row(i):
            return jnp.broadcast_to(g3[:, i:i + 1, :], g3.shape).reshape(t, dk)

        att = jnp.zeros((t, t), f32)
        for li, m in enumerate(HG_LEVELS):
            if m >= SUBLANES:
                pieces = [jnp.broadcast_to(gc[2 * p * m + m - 1:2 * p * m + m, :], (2 * m, dk))
                          for p in range(t // (2 * m))]
                bnd = pieces[0] if len(pieces) == 1 else jnp.concatenate(pieces, axis=0)
            elif m == 4:
                bnd = bcast_row(3)
            elif m == 2:
                bnd = jnp.where((rowk & 4) == 0, bcast_row(1), bcast_row(5))
            else:
                bnd = jnp.where((rowk & 1) == 0, gc, pltpu.roll(gc, 1, 0))
            upper = (rowk & m) != 0
            d = gc - bnd
            e = jnp.exp(jnp.where(upper, d, -d))
            r = (jnp.where(upper, q, k) * e).astype(bf16)
            al = lax.dot_general(r, r, nt_dims, preferred_element_type=f32)
            att = att + lm_ref[li] * al

        st = st_ref[h]
        vb = v
        intra = jnp.dot(att.astype(bf16), vb, preferred_element_type=f32)
        intra = intra + jnp.sum(q * k, axis=1, keepdims=True) * vb.astype(f32)
        qg = (q * jnp.exp(gc)).astype(bf16)
        inter = lax.dot_general(qg, st.astype(bf16), nt_dims, preferred_element_type=f32)
        glast = gc[t - 1:t, :]
        khat = (k * jnp.exp(glast - gc)).astype(bf16)
        st_ref[h] = jnp.exp(glast) * st + lax.dot_general(
            vb, khat, tn_dims, preferred_element_type=f32)

        o = intra + inter
        o = o * lax.rsqrt(jnp.mean(o * o, axis=-1, keepdims=True) + EPS) * nw_ref[0, :, sl]
        gate = g_ref[:, sl].astype(f32)
        o_ref[:, sl] = (o * (gate * _sigmoid(gate))).astype(bf16)


def _hg(z, l, bsz, seq, masks, lbs, nw):
    nc = seq // T_HG
    row = lambda b, c: b * nc + c
    lsel3 = lambda b, c: (l, 0, 0)
    zspec = lambda col: pl.BlockSpec((T_HG, W_MIX), lambda b, c: (row(b, c), col))
    vec = pl.BlockSpec((1, 1, W_MIX), lsel3)
    return pl.pallas_call(
        _hg_kernel,
        name="hgrn2_mixer",
        grid=(bsz, nc),
        in_specs=[zspec(4), zspec(5), zspec(6), zspec(7),
                  pl.BlockSpec((len(HG_LEVELS), T_HG, T_HG), lambda b, c: (0, 0, 0)),
                  vec, vec],
        out_specs=pl.BlockSpec((T_HG, W_MIX), lambda b, c: (row(b, c), 0)),
        out_shape=jax.ShapeDtypeStruct((bsz * seq, W_MIX), bf16),
        scratch_shapes=[pltpu.VMEM((HG_HEADS, HG_DK, HG_DK), f32)],
        compiler_params=_cparams(("parallel", "arbitrary")),
    )(z, z, z, z, masks, lbs, nw)


def _merge_kernel(last, ya_ref, yb_ref, yc_ref, g0_ref, g1_ref, g2_ref, x_ref, wb_ref, wo_ref,
                  nw_ref, *out_refs):
    merged = None
    for n, (y_ref, gl_ref) in enumerate(((ya_ref, g0_ref), (yb_ref, g1_ref), (yc_ref, g2_ref))):
        br = jnp.dot(y_ref[...], wb_ref[0, n], preferred_element_type=f32)
        term = _sigmoid(gl_ref[...].astype(f32)) * br
        merged = term if merged is None else merged + term
    xn = x_ref[...] + jnp.dot(merged.astype(bf16), wo_ref[0], preferred_element_type=f32)
    hn = _rms(xn, nw_ref[...])
    if last:
        out_refs[0][...] = hn
    else:
        out_refs[0][...] = xn
        out_refs[1][...] = hn.astype(bf16)


def _merge(ya, yb, yc, z, x2, l, wb, wo, nw, last, tm=256):
    m = x2.shape[0]
    yspec = pl.BlockSpec((tm, W_MIX), lambda i: (i, 0))
    gspec = lambda n: pl.BlockSpec((tm, D_MODEL), lambda i: (i, 4 + n))
    xspec = pl.BlockSpec((tm, D_MODEL), lambda i: (i, 0))
    if last:
        out_specs = [xspec]
        out_shape = [jax.ShapeDtypeStruct((m, D_MODEL), f32)]
    else:
        out_specs = [xspec, xspec]
        out_shape = [jax.ShapeDtypeStruct((m, D_MODEL), f32),
                     jax.ShapeDtypeStruct((m, D_MODEL), bf16)]
    return pl.pallas_call(
        functools.partial(_merge_kernel, last),
        name="merge_out",
        grid=(m // tm,),
        in_specs=[
            yspec, yspec, yspec, gspec(0), gspec(1), gspec(2), xspec,
            pl.BlockSpec((1, N_BRANCH, W_MIX, D_MODEL), lambda i: (l, 0, 0, 0),
                         pipeline_mode=pl.Buffered(1)),
            pl.BlockSpec((1, D_MODEL, D_MODEL), lambda i: (l, 0, 0),
                         pipeline_mode=pl.Buffered(1)),
            pl.BlockSpec((1, D_MODEL), lambda i: (0, 0)),
        ],
        out_specs=out_specs,
        out_shape=out_shape,
        compiler_params=_cparams(("parallel",)),
    )(ya, yb, yc, z, z, z, x2, wb, wo, nw)


def kernel(x, norm_w, w_in, s5_lambda_re, s5_lambda_im, s5_log_step, s5_b_re, s5_b_im, s5_c_re, s5_c_im, s5_d, s5_w_glu, s5_b_glu, rg_conv_w, rg_conv_b, rg_w_a, rg_b_a, rg_w_x, rg_b_x, rg_lambda, hg_lower_bounds, hg_norm_w, w_branch, w_out, final_norm_w):
    bsz, seq, _ = x.shape
    assert seq % T_S5 == 0 and seq % T_RG == 0 and seq % T_HG == 0
    x2 = x.reshape(bsz * seq, D_MODEL)

    tab, bbd = _s5_prep(s5_lambda_re, s5_lambda_im, s5_log_step, s5_b_re, s5_b_im)
    cre = _s5_c_layout(s5_c_re)
    cim = _s5_c_layout(s5_c_im)
    pm, pmt = _s5_segment_perm()
    wglu_bf = s5_w_glu.astype(bf16)
    wa_bd = _rg_blockdiag(rg_w_a)
    wx_bd = _rg_blockdiag(rg_w_x)
    lbs = _lbs(hg_lower_bounds.astype(f32))
    masks = _hg_level_masks()
    wb_bf = w_branch.astype(bf16)
    wo_bf = w_out.astype(bf16)
    vec3 = lambda a: a.reshape(DEPTH, 1, -1)

    h = _norm_bf16(x2, norm_w[0].reshape(1, D_MODEL))
    for l in range(DEPTH):
        last = l == DEPTH - 1
        z = _inproj(h, w_in, l)
        ya = _s5(z, l, bsz, seq, pm, pmt, bbd, cre, cim, tab, vec3(s5_d), wglu_bf,
                 vec3(s5_b_glu))
        yb = _rg(z, l, bsz, seq, rg_conv_w, vec3(rg_conv_b), wa_bd, wx_bd,
                 vec3(rg_b_a), vec3(rg_b_x), vec3(rg_lambda))
        yc = _hg(z, l, bsz, seq, masks, vec3(lbs), vec3(hg_norm_w))
        nw = (final_norm_w if last else norm_w[l + 1]).reshape(1, D_MODEL)
        outs = _merge(ya, yb, yc, z, x2, l, wb_bf, wo_bf, nw, last)
        if last:
            out = outs[0]
        else:
            x2, h = outs

    return out.reshape(bsz, seq, D_MODEL)
```

```python
import functools

import numpy as np
import jax
import jax.numpy as jnp
from jax import lax
from jax.experimental import pallas as pl
from jax.experimental.pallas import tpu as pltpu

D_MODEL = 2048
DEPTH = 4
N_BRANCH = 3
W_MIX = D_MODEL // 2
S5_GROUP = 16
S5_GROUPS = W_MIX // S5_GROUP
S5_STATE = 64
RG_BLOCKS = 16
RG_BLOCK = W_MIX // RG_BLOCKS
RG_CONV = 4
RG_C = 8.0
HG_HEADS = 8
HG_DK = W_MIX // HG_HEADS
EPS = 1e-6
N_IN = 8 * W_MIX + N_BRANCH * D_MODEL

LANES = 128
SUBLANES = 8
BF16_ROWS = 16
VMEM_LIMIT = 56 * 1024 * 1024

S5_NJ = W_MIX // LANES
S5_HALF = (LANES // S5_GROUP) * S5_STATE
S5_KS_STEPS = 3
S5_NTAB = S5_KS_STEPS + 2

T_S5 = 256
S5_SEG = T_S5 // SUBLANES
T_RG = 256
T_HG = 128
HG_LEVELS = tuple(T_HG >> (i + 1) for i in range(T_HG.bit_length() - 1))
RG_DIAG = 256
RG_ND = W_MIX // RG_DIAG

bf16 = jnp.bfloat16
f32 = jnp.float32


def _log2(n):
    assert n & (n - 1) == 0
    return n.bit_length() - 1


def _sigmoid(x):
    return 1.0 / (1.0 + jnp.exp(-x))


def _cparams(sem):
    return pltpu.CompilerParams(dimension_semantics=sem, vmem_limit_bytes=VMEM_LIMIT)


def _rms(x, w):
    ms = jnp.mean(x * x, axis=-1, keepdims=True)
    return x * lax.rsqrt(ms + EPS) * w


def _norm_kernel(x_ref, w_ref, o_ref):
    o_ref[...] = _rms(x_ref[...], w_ref[...]).astype(bf16)


def _norm_bf16(x2, w, tm=512):
    m = x2.shape[0]
    return pl.pallas_call(
        _norm_kernel,
        name="first_norm",
        grid=(m // tm,),
        in_specs=[pl.BlockSpec((tm, D_MODEL), lambda i: (i, 0)),
                  pl.BlockSpec((1, D_MODEL), lambda i: (0, 0))],
        out_specs=pl.BlockSpec((tm, D_MODEL), lambda i: (i, 0)),
        out_shape=jax.ShapeDtypeStruct((m, D_MODEL), bf16),
        compiler_params=_cparams(("parallel",)),
    )(x2, w)


def _inproj_kernel(h_ref, w_ref, o_ref, wbf_ref):
    @pl.when(pl.program_id(1) == 0)
    def _():
        wbf_ref[...] = w_ref[0].astype(bf16)

    o_ref[...] = jnp.dot(h_ref[...], wbf_ref[...], preferred_element_type=f32).astype(bf16)


def _inproj(h, w_in, l, tm=2048, tn=1024):
    m = h.shape[0]
    return pl.pallas_call(
        _inproj_kernel,
        name="inproj",
        grid=(N_IN // tn, m // tm),
        in_specs=[
            pl.BlockSpec((tm, D_MODEL), lambda j, i: (i, 0)),
            pl.BlockSpec((1, D_MODEL, tn), lambda j, i: (l, 0, j)),
        ],
        out_specs=pl.BlockSpec((tm, tn), lambda j, i: (i, j)),
        out_shape=jax.ShapeDtypeStruct((m, N_IN), bf16),
        scratch_shapes=[pltpu.VMEM((D_MODEL, tn), bf16)],
        compiler_params=_cparams(("arbitrary", "arbitrary")),
    )(h, w_in)


def _s5_prep_kernel(lr_ref, li_ref, ls_ref, lrb_ref, lib_ref, lsb_ref, br_ref, bi_ref,
                    tre_ref, tim_ref, bbr_ref, bbi_ref):
    lr = lr_ref[0]
    li = li_ref[0]
    step = jnp.exp(ls_ref[0])
    gp = lr.shape[-1]

    def power(n):
        mag = jnp.exp(lr * step * n)
        ang = li * step * n
        return mag * jnp.cos(ang), mag * jnp.sin(ang)

    row = lax.broadcasted_iota(jnp.int32, (SUBLANES, gp), 0)
    for k in range(S5_NTAB):
        keep = row >= 0
        if k < S5_KS_STEPS:
            n = jnp.full((SUBLANES, gp), float(S5_SEG << k), f32)
            keep = row >= (1 << k)
        elif k == S5_KS_STEPS:
            n = ((row + 1) * S5_SEG).astype(f32)
        else:
            n = jnp.full((SUBLANES, gp), 1.0, f32)
        pr, pi = power(n)
        tre_ref[0, k] = jnp.where(keep, pr, 0.0)
        tim_ref[0, k] = jnp.where(keep, pi, 0.0)
    lrb = lrb_ref[0]
    lib = lib_ref[0]
    stepb = jnp.exp(lsb_ref[0])
    mag = jnp.exp(lrb * stepb)
    ang = lib * stepb
    num_re = mag * jnp.cos(ang) - 1.0
    num_im = mag * jnp.sin(ang)
    den = lrb * lrb + lib * lib
    coef_re = (num_re * lrb + num_im * lib) / den
    coef_im = (num_im * lrb - num_re * lib) / den
    b_re = br_ref[0]
    b_im = bi_ref[0]
    bbr_ref[0] = coef_re * b_re - coef_im * b_im
    bbi_ref[0] = coef_re * b_im + coef_im * b_re


def _s5_prep(lam_re, lam_im, log_step, b_re, b_im):
    gp = S5_GROUPS * S5_STATE
    ph = S5_STATE * S5_GROUP
    nt = S5_NTAB
    flat = lambda a: a.reshape(DEPTH, 1, gp)
    ls_f = jnp.broadcast_to(log_step[:, :, None], (DEPTH, S5_GROUPS, S5_STATE))
    rep = lambda a: jnp.repeat(a, S5_GROUP, axis=2)
    ls_b = jnp.broadcast_to(log_step[:, :, None], (DEPTH, S5_GROUPS, ph))
    spec_f = pl.BlockSpec((1, 1, gp), lambda l: (l, 0, 0))
    spec_b = pl.BlockSpec((1, S5_GROUPS, ph), lambda l: (l, 0, 0))
    spec_t = pl.BlockSpec((1, nt, SUBLANES, gp), lambda l: (l, 0, 0, 0))
    tre, tim, bbr, bbi = pl.pallas_call(
        _s5_prep_kernel,
        name="s5_prep",
        grid=(DEPTH,),
        in_specs=[spec_f, spec_f, spec_f, spec_b, spec_b, spec_b, spec_b, spec_b],
        out_specs=[spec_t, spec_t, spec_b, spec_b],
        out_shape=[
            jax.ShapeDtypeStruct((DEPTH, nt, SUBLANES, gp), f32),
            jax.ShapeDtypeStruct((DEPTH, nt, SUBLANES, gp), f32),
            jax.ShapeDtypeStruct((DEPTH, S5_GROUPS, ph), f32),
            jax.ShapeDtypeStruct((DEPTH, S5_GROUPS, ph), f32),
        ],
        compiler_params=_cparams(("parallel",)),
    )(flat(lam_re), flat(lam_im), flat(ls_f), rep(lam_re), rep(lam_im), ls_b,
      b_re.reshape(DEPTH, S5_GROUPS, ph), b_im.reshape(DEPTH, S5_GROUPS, ph))

    def interleave(re, im):
        lead = re.shape[:-1]
        both = jnp.stack([re.reshape(*lead, S5_NJ, S5_HALF), im.reshape(*lead, S5_NJ, S5_HALF)],
                         axis=-2)
        return both.reshape(*lead, S5_NJ * 2 * S5_HALF)

    tab = interleave(tre, tim)
    gl = LANES // S5_GROUP
    eye = jnp.eye(gl, dtype=f32)
    bb = jnp.stack([bbr, bbi], axis=1).reshape(DEPTH, 2, S5_NJ, gl, S5_STATE, S5_GROUP)
    bbd = jnp.einsum('lcjgph,gk->ljghckp', bb, eye)
    bbd = bbd.reshape(DEPTH, S5_NJ, LANES, 2 * S5_HALF).astype(bf16)
    return tab, bbd


def _s5_c_layout(c):
    gl = LANES // S5_GROUP
    eye = jnp.eye(gl, dtype=f32)
    c6 = c.reshape(DEPTH, S5_NJ, gl, S5_GROUP, S5_STATE)
    cbd = jnp.einsum('ljghp,gk->ljgpkh', c6, eye)
    return cbd.reshape(DEPTH, S5_NJ, S5_HALF, LANES).astype(bf16)


def _s5_segment_perm():
    pm = np.zeros((T_S5, T_S5), np.float32)
    for r in range(SUBLANES):
        for t in range(S5_SEG):
            pm[t * SUBLANES + r, r * S5_SEG + t] = 1.0
    return jnp.asarray(pm, bf16), jnp.asarray(pm.T, bf16)


def _lbs_kernel(x_ref, o_ref):
    rows = [x_ref[i:i + 1, :] for i in range(DEPTH)]
    m = functools.reduce(jnp.maximum, rows)
    es = [jnp.exp(r - m) for r in rows]
    tot = functools.reduce(lambda a, b: a + b, es)
    ps = [e / tot for e in es]
    acc = ps[0]
    out = [acc - ps[0]]
    for i in range(1, DEPTH):
        acc = acc + ps[i]
        out.append(acc - ps[0])
    o_ref[...] = jnp.concatenate(out, axis=0)


def _lbs(hg_lower_bounds):
    return pl.pallas_call(
        _lbs_kernel,
        name="hg_lower_bounds",
        out_shape=jax.ShapeDtypeStruct((DEPTH, W_MIX), f32),
    )(hg_lower_bounds)


def _s5_kernel(u_ref, ga_ref, pm_ref, pmt_ref, bbd_ref, cre_ref, cim_ref, tab_ref,
               d_ref, wglu_ref, bglu_ref, o_ref, st_ref, bu_ref, xb_ref, y_ref):
    hw = S5_HALF
    r8 = SUBLANES

    @pl.when(pl.program_id(1) == 0)
    def _():
        st_ref[...] = jnp.zeros_like(st_ref)

    pm = pm_ref[...]
    up = jnp.dot(pm, u_ref[...], preferred_element_type=f32)
    gap = jnp.dot(pm, ga_ref[...], preferred_element_type=f32)
    up_bf = up.astype(bf16)
    row8 = lax.broadcasted_iota(jnp.int32, (r8, hw), 0)

    for j in range(S5_NJ):
        slot = j
        lanes = slice(j * LANES, (j + 1) * LANES)
        tl = slice(j * 2 * hw, (j + 1) * 2 * hw)
        bu_ref[slot] = jnp.dot(up_bf[:, lanes], bbd_ref[0, j], preferred_element_type=f32)
        a1 = tab_ref[0, S5_KS_STEPS + 1, :, tl]
        are = a1[:, :hw]
        aim = a1[:, hw:]

        xre = bu_ref[slot, 0:r8, :hw]
        xim = bu_ref[slot, 0:r8, hw:]
        for t in range(1, S5_SEG):
            rows = slice(t * r8, (t + 1) * r8)
            xre, xim = (are * xre - aim * xim + bu_ref[slot, rows, :hw],
                        are * xim + aim * xre + bu_ref[slot, rows, hw:])
            bu_ref[slot, rows, :hw] = xre
            bu_ref[slot, rows, hw:] = xim

        for k in range(S5_KS_STEPS):
            mk = tab_ref[0, k, :, tl]
            mre = mk[:, :hw]
            mim = mk[:, hw:]
            sre = pltpu.roll(xre, 1 << k, 0)
            sim = pltpu.roll(xim, 1 << k, 0)
            xre, xim = xre + mre * sre - mim * sim, xim + mre * sim + mim * sre
        ak = tab_ref[0, S5_KS_STEPS, :, tl]
        cre = st_ref[j:j + 1, :hw]
        cim = st_ref[j:j + 1, hw:]
        fre = xre + ak[:, :hw] * cre - ak[:, hw:] * cim
        fim = xim + ak[:, :hw] * cim + ak[:, hw:] * cre
        st_ref[j:j + 1, :hw] = fre[r8 - 1:, :]
        st_ref[j:j + 1, hw:] = fim[r8 - 1:, :]
        ire = jnp.where(row8 == 0, cre, pltpu.roll(fre, 1, 0))
        iim = jnp.where(row8 == 0, cim, pltpu.roll(fim, 1, 0))

        wre, wim = ire, iim
        for t2 in range(S5_SEG // 2):
            rows = slice(t2 * BF16_ROWS, (t2 + 1) * BF16_ROWS)
            wre0, wim0 = are * wre - aim * wim, are * wim + aim * wre
            wre, wim = are * wre0 - aim * wim0, are * wim0 + aim * wre0
            xr = bu_ref[slot, rows, :hw] + jnp.concatenate([wre0, wre], axis=0)
            xi = bu_ref[slot, rows, hw:] + jnp.concatenate([wim0, wim], axis=0)
            xb_ref[slot, rows, :hw] = xr.astype(bf16)
            xb_ref[slot, rows, hw:] = xi.astype(bf16)

        y = (jnp.dot(xb_ref[slot, :, :hw], cre_ref[0, j], preferred_element_type=f32)
             - jnp.dot(xb_ref[slot, :, hw:], cim_ref[0, j], preferred_element_type=f32))
        y = y + d_ref[0, :, lanes] * up[:, lanes]
        y_ref[:, lanes] = jax.nn.gelu(y)

    y = y_ref[...]
    gl = jnp.dot(y.astype(bf16), wglu_ref[0], preferred_element_type=f32) + bglu_ref[0]
    outp = (y * _sigmoid(gl) * (gap * _sigmoid(gap))).astype(bf16)
    o_ref[...] = jnp.dot(pmt_ref[...], outp, preferred_element_type=f32).astype(bf16)


def _s5(z, l, bsz, seq, pm, pmt, bbd, cre, cim, tab, d, wglu, bglu):
    nc = seq // T_S5
    row = lambda b, c: b * nc + c
    lsel3 = lambda b, c: (l, 0, 0)
    lsel4 = lambda b, c: (l, 0, 0, 0)
    perm = pl.BlockSpec((T_S5, T_S5), lambda b, c: (0, 0))
    return pl.pallas_call(
        _s5_kernel,
        name="s5_mixer",
        grid=(bsz, nc),
        in_specs=[
            pl.BlockSpec((T_S5, W_MIX), lambda b, c: (row(b, c), 0)),
            pl.BlockSpec((T_S5, W_MIX), lambda b, c: (row(b, c), 1)),
            perm, perm,
            pl.BlockSpec((1, S5_NJ, LANES, 2 * S5_HALF), lsel4),
            pl.BlockSpec((1, S5_NJ, S5_HALF, LANES), lsel4),
            pl.BlockSpec((1, S5_NJ, S5_HALF, LANES), lsel4),
            pl.BlockSpec((1, S5_NTAB, SUBLANES, S5_NJ * 2 * S5_HALF), lsel4),
            pl.BlockSpec((1, 1, W_MIX), lsel3),
            pl.BlockSpec((1, W_MIX, W_MIX), lsel3),
            pl.BlockSpec((1, 1, W_MIX), lsel3),
        ],
        out_specs=pl.BlockSpec((T_S5, W_MIX), lambda b, c: (row(b, c), 0)),
        out_shape=jax.ShapeDtypeStruct((bsz * seq, W_MIX), bf16),
        scratch_shapes=[
            pltpu.VMEM((S5_NJ, 2 * S5_HALF), f32),
            pltpu.VMEM((S5_NJ, T_S5, 2 * S5_HALF), f32),
            pltpu.VMEM((S5_NJ, T_S5, 2 * S5_HALF), bf16),
            pltpu.VMEM((T_S5, W_MIX), f32),
        ],
        compiler_params=_cparams(("parallel", "arbitrary")),
    )(z, z, pm, pmt, bbd, cre, cim, tab, d, wglu, bglu)


def _rg_kernel(x_ref, g_ref, cw_ref, cb_ref, wa_ref, wx_ref, ba_ref, bx_ref, lam_ref,
               o_ref, tail_ref, h_ref, xe_ref, a_ref, b_ref):
    t = x_ref.shape[0]
    c = pl.program_id(1)

    @pl.when(c == 0)
    def _():
        tail_ref[...] = jnp.zeros_like(tail_ref)
        h_ref[...] = jnp.zeros_like(h_ref)

    x = x_ref[...].astype(f32)
    xe_ref[0:SUBLANES, :] = tail_ref[...]
    xe_ref[SUBLANES:SUBLANES + t, :] = x
    tail_ref[...] = x[t - SUBLANES:t, :]
    xc = cb_ref[0]
    for k in range(RG_CONV):
        off = SUBLANES - (RG_CONV - 1) + k
        xc = xc + cw_ref[0, k:k + 1, :] * xe_ref[off:off + t, :]

    rs, ins = [], []
    for m in range(RG_ND):
        sl = slice(m * RG_DIAG, (m + 1) * RG_DIAG)
        xs = xc[:, sl].astype(bf16)
        rs.append(jnp.dot(xs, wa_ref[0, m], preferred_element_type=f32))
        ins.append(jnp.dot(xs, wx_ref[0, m], preferred_element_type=f32))
    r = _sigmoid(jnp.concatenate(rs, axis=1) + ba_ref[0])
    i = _sigmoid(jnp.concatenate(ins, axis=1) + bx_ref[0])

    nlam = -lam_ref[0]
    softplus = jnp.maximum(nlam, 0.0) + jnp.log1p(jnp.exp(-jnp.abs(nlam)))
    log_a = (-RG_C * softplus) * r
    a = jnp.exp(log_a)
    mult = jnp.sqrt(-jnp.tanh(log_a) * (a * a + 1.0))
    row = lax.broadcasted_iota(jnp.int32, (t, W_MIX), 0)
    mult = jnp.where(jnp.logical_and(row == 0, c == 0), 1.0, mult)
    a_ref[...] = a
    b_ref[...] = mult * (i * xc)

    row8 = lax.broadcasted_iota(jnp.int32, (SUBLANES, W_MIX), 0)

    def body(rb, carry):
        row0 = pl.multiple_of(rb * SUBLANES, SUBLANES)
        av = a_ref[pl.ds(row0, SUBLANES), :]
        bv = b_ref[pl.ds(row0, SUBLANES), :]
        for k in range(3):
            s = 1 << k
            keep = row8 >= s
            sa = jnp.where(keep, pltpu.roll(av, s, 0), 1.0)
            sb = jnp.where(keep, pltpu.roll(bv, s, 0), 0.0)
            bv = bv + av * sb
            av = av * sa
        hv = bv + av * carry
        b_ref[pl.ds(row0, SUBLANES), :] = hv
        return hv[SUBLANES - 1:, :]

    carry = lax.fori_loop(0, t // SUBLANES, body, h_ref[...])
    h_ref[...] = carry
    g = g_ref[...].astype(f32)
    o_ref[...] = (b_ref[...] * (g * _sigmoid(g))).astype(bf16)


def _rg(z, l, bsz, seq, cw, cb, wa_bd, wx_bd, ba, bx, lam):
    nc = seq // T_RG
    row = lambda b, c: b * nc + c
    lsel3 = lambda b, c: (l, 0, 0)
    lsel4 = lambda b, c: (l, 0, 0, 0)
    vec = pl.BlockSpec((1, 1, W_MIX), lsel3)
    return pl.pallas_call(
        _rg_kernel,
        name="rglru_mixer",
        grid=(bsz, nc),
        in_specs=[
            pl.BlockSpec((T_RG, W_MIX), lambda b, c: (row(b, c), 2)),
            pl.BlockSpec((T_RG, W_MIX), lambda b, c: (row(b, c), 3)),
            pl.BlockSpec((1, RG_CONV, W_MIX), lsel3),
            vec,
            pl.BlockSpec((1, RG_ND, RG_DIAG, RG_DIAG), lsel4),
            pl.BlockSpec((1, RG_ND, RG_DIAG, RG_DIAG), lsel4),
            vec, vec, vec,
        ],
        out_specs=pl.BlockSpec((T_RG, W_MIX), lambda b, c: (row(b, c), 0)),
        out_shape=jax.ShapeDtypeStruct((bsz * seq, W_MIX), bf16),
        scratch_shapes=[
            pltpu.VMEM((SUBLANES, W_MIX), f32),
            pltpu.VMEM((1, W_MIX), f32),
            pltpu.VMEM((T_RG + SUBLANES, W_MIX), f32),
            pltpu.VMEM((T_RG, W_MIX), f32),
            pltpu.VMEM((T_RG, W_MIX), f32),
        ],
        compiler_params=_cparams(("parallel", "arbitrary")),
    )(z, z, cw, cb, wa_bd, wx_bd, ba, bx, lam)


def _rg_blockdiag(w):
    per = RG_DIAG // RG_BLOCK
    eye = jnp.eye(per, dtype=f32)
    w5 = w.reshape(DEPTH, RG_ND, per, RG_BLOCK, RG_BLOCK)
    bd = jnp.einsum('lmnij,nk->lmnikj', w5, eye)
    return bd.reshape(DEPTH, RG_ND, RG_DIAG, RG_DIAG).astype(bf16)


def _hg_level_masks():
    t = np.arange(T_HG)[:, None]
    s = np.arange(T_HG)[None, :]
    masks = [((t // (2 * m)) == (s // (2 * m))) & ((t & m) != 0) & ((s & m) == 0)
             for m in HG_LEVELS]
    return jnp.asarray(np.stack(masks).astype(np.float32))


def _hg_kernel(q_ref, f_ref, i_ref, g_ref, lm_ref, lb_ref, nw_ref, o_ref,
               st_ref, q_s, k_s, gc_s, r_s, att_s):
    t = q_ref.shape[0]
    w = q_ref.shape[1]
    dk = HG_DK
    nb = t // SUBLANES

    @pl.when(pl.program_id(1) == 0)
    def _():
        st_ref[...] = jnp.zeros_like(st_ref)

    row = lax.broadcasted_iota(jnp.int32, (t, t), 0)
    col = lax.broadcasted_iota(jnp.int32, (t, t), 1)
    tri = (row >= col).astype(f32)
    roww = lax.broadcasted_iota(jnp.int32, (t, w), 0)
    nt_dims = (((1,), (1,)), ((), ()))
    tn_dims = (((0,), (0,)), ((), ()))

    qr = q_ref[...].astype(f32)
    lb = lb_ref[0]
    fg = lb + (1.0 - lb) * _sigmoid(f_ref[...].astype(f32))
    q_s[...] = qr * _sigmoid(qr)
    k_s[...] = 1.0 - fg
    gc_s[...] = jnp.dot(tri, jnp.log(fg), preferred_element_type=f32,
                        precision=lax.Precision.HIGHEST)

    for li, m in enumerate(HG_LEVELS):
        if m >= SUBLANES:
            pieces = []
            for p in range(t // (2 * m)):
                lo = slice(2 * p * m, 2 * p * m + m)
                up = slice(2 * p * m + m, 2 * p * m + 2 * m)
                b = gc_s[2 * p * m + m - 1:2 * p * m + m, :]
                pieces.append(k_s[lo, :] * jnp.exp(b - gc_s[lo, :]))
                pieces.append(q_s[up, :] * jnp.exp(gc_s[up, :] - b))
            r = jnp.concatenate(pieces, axis=0)
        else:
            gc = gc_s[...]
            g3 = gc.reshape(nb, SUBLANES, w)

            def bcast_row(i, g3=g3):
                return jnp.broadcast_to(g3[:, i:i + 1, :], g3.shape).reshape(t, w)

            if m == 4:
                bnd = bcast_row(3)
            elif m == 2:
                bnd = jnp.where((roww & 4) == 0, bcast_row(1), bcast_row(5))
            else:
                bnd = jnp.where((roww & 1) == 0, gc, pltpu.roll(gc, 1, 0))
            upper = (roww & m) != 0
            d = gc - bnd
            e = jnp.exp(jnp.where(upper, d, -d))
            r = jnp.where(upper, q_s[...], k_s[...]) * e
        r_s[...] = r.astype(bf16)
        for h in range(HG_HEADS):
            rh = r_s[:, h * dk:(h + 1) * dk]
            al = lm_ref[li] * lax.dot_general(rh, rh, nt_dims, preferred_element_type=f32)
            if li == 0:
                att_s[h] = al
            else:
                att_s[h] += al

    for h in range(HG_HEADS):
        sl = slice(h * dk, (h + 1) * dk)
        q = q_s[:, sl]
        k = k_s[:, sl]
        gc = gc_s[:, sl]
        vb = i_ref[:, sl]
        st = st_ref[h]
        intra = jnp.dot(att_s[h].astype(bf16), vb, preferred_element_type=f32)
        intra = intra + jnp.sum(q * k, axis=1, keepdims=True) * vb.astype(f32)
        qg = (q * jnp.exp(gc)).astype(bf16)
        inter = lax.dot_general(qg, st.astype(bf16), nt_dims, preferred_element_type=f32)
        glast = gc[t - 1:t, :]
        khat = (k * jnp.exp(glast - gc)).astype(bf16)
        st_ref[h] = jnp.exp(glast) * st + lax.dot_general(
            vb, khat, tn_dims, preferred_element_type=f32)

        o = intra + inter
        o = o * lax.rsqrt(jnp.mean(o * o, axis=-1, keepdims=True) + EPS) * nw_ref[0, :, sl]
        gate = g_ref[:, sl].astype(f32)
        o_ref[:, sl] = (o * (gate * _sigmoid(gate))).astype(bf16)


def _hg(z, l, bsz, seq, masks, lbs, nw):
    nc = seq // T_HG
    row = lambda b, c: b * nc + c
    lsel3 = lambda b, c: (l, 0, 0)
    zspec = lambda col: pl.BlockSpec((T_HG, W_MIX), lambda b, c: (row(b, c), col))
    vec = pl.BlockSpec((1, 1, W_MIX), lsel3)
    return pl.pallas_call(
        _hg_kernel,
        name="hgrn2_mixer",
        grid=(bsz, nc),
        in_specs=[zspec(4), zspec(5), zspec(6), zspec(7),
                  pl.BlockSpec((len(HG_LEVELS), T_HG, T_HG), lambda b, c: (0, 0, 0)),
                  vec, vec],
        out_specs=pl.BlockSpec((T_HG, W_MIX), lambda b, c: (row(b, c), 0)),
        out_shape=jax.ShapeDtypeStruct((bsz * seq, W_MIX), bf16),
        scratch_shapes=[
            pltpu.VMEM((HG_HEADS, HG_DK, HG_DK), f32),
            pltpu.VMEM((T_HG, W_MIX), f32),
            pltpu.VMEM((T_HG, W_MIX), f32),
            pltpu.VMEM((T_HG, W_MIX), f32),
            pltpu.VMEM((T_HG, W_MIX), bf16),
            pltpu.VMEM((HG_HEADS, T_HG, T_HG), f32),
        ],
        compiler_params=_cparams(("parallel", "arbitrary")),
    )(z, z, z, z, masks, lbs, nw)


def _merge_kernel(last, ya_ref, yb_ref, yc_ref, g0_ref, g1_ref, g2_ref, x_ref, wb_ref, wo_ref,
                  nw_ref, *out_refs):
    merged = None
    for n, (y_ref, gl_ref) in enumerate(((ya_ref, g0_ref), (yb_ref, g1_ref), (yc_ref, g2_ref))):
        br = jnp.dot(y_ref[...], wb_ref[0, n], preferred_element_type=f32)
        term = _sigmoid(gl_ref[...].astype(f32)) * br
        merged = term if merged is None else merged + term
    xn = x_ref[...] + jnp.dot(merged.astype(bf16), wo_ref[0], preferred_element_type=f32)
    hn = _rms(xn, nw_ref[...])
    if last:
        out_refs[0][...] = hn
    else:
        out_refs[0][...] = xn
        out_refs[1][...] = hn.astype(bf16)


def _merge(ya, yb, yc, z, x2, l, wb, wo, nw, last, tm=256):
    m = x2.shape[0]
    yspec = pl.BlockSpec((tm, W_MIX), lambda i: (i, 0))
    gspec = lambda n: pl.BlockSpec((tm, D_MODEL), lambda i: (i, 4 + n))
    xspec = pl.BlockSpec((tm, D_MODEL), lambda i: (i, 0))
    if last:
        out_specs = [xspec]
        out_shape = [jax.ShapeDtypeStruct((m, D_MODEL), f32)]
    else:
        out_specs = [xspec, xspec]
        out_shape = [jax.ShapeDtypeStruct((m, D_MODEL), f32),
                     jax.ShapeDtypeStruct((m, D_MODEL), bf16)]
    return pl.pallas_call(
        functools.partial(_merge_kernel, last),
        name="merge_out",
        grid=(m // tm,),
        in_specs=[
            yspec, yspec, yspec, gspec(0), gspec(1), gspec(2), xspec,
            pl.BlockSpec((1, N_BRANCH, W_MIX, D_MODEL), lambda i: (l, 0, 0, 0),
                         pipeline_mode=pl.Buffered(1)),
            pl.BlockSpec((1, D_MODEL, D_MODEL), lambda i: (l, 0, 0),
                         pipeline_mode=pl.Buffered(1)),
            pl.BlockSpec((1, D_MODEL), lambda i: (0, 0)),
        ],
        out_specs=out_specs,
        out_shape=out_shape,
        compiler_params=_cparams(("parallel",)),
    )(ya, yb, yc, z, z, z, x2, wb, wo, nw)


def kernel(x, norm_w, w_in, s5_lambda_re, s5_lambda_im, s5_log_step, s5_b_re, s5_b_im, s5_c_re, s5_c_im, s5_d, s5_w_glu, s5_b_glu, rg_conv_w, rg_conv_b, rg_w_a, rg_b_a, rg_w_x, rg_b_x, rg_lambda, hg_lower_bounds, hg_norm_w, w_branch, w_out, final_norm_w):
    bsz, seq, _ = x.shape
    assert seq % T_S5 == 0 and seq % T_RG == 0 and seq % T_HG == 0
    x2 = x.reshape(bsz * seq, D_MODEL)

    tab, bbd = _s5_prep(s5_lambda_re, s5_lambda_im, s5_log_step, s5_b_re, s5_b_im)
    cre = _s5_c_layout(s5_c_re)
    cim = _s5_c_layout(s5_c_im)
    pm, pmt = _s5_segment_perm()
    wglu_bf = s5_w_glu.astype(bf16)
    wa_bd = _rg_blockdiag(rg_w_a)
    wx_bd = _rg_blockdiag(rg_w_x)
    lbs = _lbs(hg_lower_bounds.astype(f32))
    masks = _hg_level_masks()
    wb_bf = w_branch.astype(bf16)
    wo_bf = w_out.astype(bf16)
    vec3 = lambda a: a.reshape(DEPTH, 1, -1)

    h = _norm_bf16(x2, norm_w[0].reshape(1, D_MODEL))
    for l in range(DEPTH):
        last = l == DEPTH - 1
        z = _inproj(h, w_in, l)
        ya = _s5(z, l, bsz, seq, pm, pmt, bbd, cre, cim, tab, vec3(s5_d), wglu_bf,
                 vec3(s5_b_glu))
        yb = _rg(z, l, bsz, seq, rg_conv_w, vec3(rg_conv_b), wa_bd, wx_bd,
                 vec3(rg_b_a), vec3(rg_b_x), vec3(rg_lambda))
        yc = _hg(z, l, bsz, seq, masks, vec3(lbs), vec3(hg_norm_w))
        nw = (final_norm_w if last else norm_w[l + 1]).reshape(1, D_MODEL)
        outs = _merge(ya, yb, yc, z, x2, l, wb_bf, wo_bf, nw, last)
        if last:
            out = outs[0]
        else:
            x2, h = outs

    return out.reshape(bsz, seq, D_MODEL)
```

```python
import functools

import numpy as np
import jax
import jax.numpy as jnp
from jax import lax
from jax.experimental import pallas as pl
from jax.experimental.pallas import tpu as pltpu

D_MODEL = 2048
DEPTH = 4
N_BRANCH = 3
W_MIX = D_MODEL // 2
S5_GROUP = 16
S5_GROUPS = W_MIX // S5_GROUP
S5_STATE = 64
RG_BLOCKS = 16
RG_BLOCK = W_MIX // RG_BLOCKS
RG_CONV = 4
RG_C = 8.0
HG_HEADS = 8
HG_DK = W_MIX // HG_HEADS
EPS = 1e-6
N_IN = 8 * W_MIX + N_BRANCH * D_MODEL

LANES = 128
SUBLANES = 8
BF16_ROWS = 16
VMEM_LIMIT = 56 * 1024 * 1024

S5_NJ = W_MIX // LANES
S5_HALF = (LANES // S5_GROUP) * S5_STATE
S5_KS_STEPS = 3
S5_NTAB = S5_KS_STEPS + 2

T_S5 = 512
S5_SEG = T_S5 // SUBLANES
T_RG = 256
T_HG = 128
HG_LEVELS = tuple(T_HG >> (i + 1) for i in range(T_HG.bit_length() - 1))
RG_DIAG = 256
RG_ND = W_MIX // RG_DIAG

bf16 = jnp.bfloat16
f32 = jnp.float32


def _log2(n):
    assert n & (n - 1) == 0
    return n.bit_length() - 1


def _sigmoid(x):
    return 1.0 / (1.0 + jnp.exp(-x))


def _cparams(sem):
    return pltpu.CompilerParams(dimension_semantics=sem, vmem_limit_bytes=VMEM_LIMIT)


def _rms(x, w):
    ms = jnp.mean(x * x, axis=-1, keepdims=True)
    return x * lax.rsqrt(ms + EPS) * w


def _norm_kernel(x_ref, w_ref, o_ref):
    o_ref[...] = _rms(x_ref[...], w_ref[...]).astype(bf16)


def _norm_bf16(x2, w, tm=512):
    m = x2.shape[0]
    return pl.pallas_call(
        _norm_kernel,
        name="first_norm",
        grid=(m // tm,),
        in_specs=[pl.BlockSpec((tm, D_MODEL), lambda i: (i, 0)),
                  pl.BlockSpec((1, D_MODEL), lambda i: (0, 0))],
        out_specs=pl.BlockSpec((tm, D_MODEL), lambda i: (i, 0)),
        out_shape=jax.ShapeDtypeStruct((m, D_MODEL), bf16),
        compiler_params=_cparams(("parallel",)),
    )(x2, w)


def _inproj_kernel(h_ref, w_ref, o_ref, wbf_ref):
    @pl.when(pl.program_id(1) == 0)
    def _():
        wbf_ref[...] = w_ref[0].astype(bf16)

    o_ref[...] = jnp.dot(h_ref[...], wbf_ref[...], preferred_element_type=f32).astype(bf16)


def _inproj(h, w_in, l, tm=2048, tn=1024):
    m = h.shape[0]
    return pl.pallas_call(
        _inproj_kernel,
        name="inproj",
        grid=(N_IN // tn, m // tm),
        in_specs=[
            pl.BlockSpec((tm, D_MODEL), lambda j, i: (i, 0)),
            pl.BlockSpec((1, D_MODEL, tn), lambda j, i: (l, 0, j)),
        ],
        out_specs=pl.BlockSpec((tm, tn), lambda j, i: (i, j)),
        out_shape=jax.ShapeDtypeStruct((m, N_IN), bf16),
        scratch_shapes=[pltpu.VMEM((D_MODEL, tn), bf16)],
        compiler_params=_cparams(("arbitrary", "arbitrary")),
    )(h, w_in)


def _s5_prep_kernel(lr_ref, li_ref, ls_ref, lrb_ref, lib_ref, lsb_ref, br_ref, bi_ref,
                    tre_ref, tim_ref, bbr_ref, bbi_ref):
    lr = lr_ref[0]
    li = li_ref[0]
    step = jnp.exp(ls_ref[0])
    gp = lr.shape[-1]

    def power(n):
        mag = jnp.exp(lr * step * n)
        ang = li * step * n
        return mag * jnp.cos(ang), mag * jnp.sin(ang)

    row = lax.broadcasted_iota(jnp.int32, (SUBLANES, gp), 0)
    for k in range(S5_NTAB):
        keep = row >= 0
        if k < S5_KS_STEPS:
            n = jnp.full((SUBLANES, gp), float(S5_SEG << k), f32)
            keep = row >= (1 << k)
        elif k == S5_KS_STEPS:
            n = ((row + 1) * S5_SEG).astype(f32)
        else:
            n = jnp.full((SUBLANES, gp), 1.0, f32)
        pr, pi = power(n)
        tre_ref[0, k] = jnp.where(keep, pr, 0.0)
        tim_ref[0, k] = jnp.where(keep, pi, 0.0)
    lrb = lrb_ref[0]
    lib = lib_ref[0]
    stepb = jnp.exp(lsb_ref[0])
    mag = jnp.exp(lrb * stepb)
    ang = lib * stepb
    num_re = mag * jnp.cos(ang) - 1.0
    num_im = mag * jnp.sin(ang)
    den = lrb * lrb + lib * lib
    coef_re = (num_re * lrb + num_im * lib) / den
    coef_im = (num_im * lrb - num_re * lib) / den
    b_re = br_ref[0]
    b_im = bi_ref[0]
    bbr_ref[0] = coef_re * b_re - coef_im * b_im
    bbi_ref[0] = coef_re * b_im + coef_im * b_re


def _s5_prep(lam_re, lam_im, log_step, b_re, b_im):
    gp = S5_GROUPS * S5_STATE
    ph = S5_STATE * S5_GROUP
    nt = S5_NTAB
    flat = lambda a: a.reshape(DEPTH, 1, gp)
    ls_f = jnp.broadcast_to(log_step[:, :, None], (DEPTH, S5_GROUPS, S5_STATE))
    rep = lambda a: jnp.repeat(a, S5_GROUP, axis=2)
    ls_b = jnp.broadcast_to(log_step[:, :, None], (DEPTH, S5_GROUPS, ph))
    spec_f = pl.BlockSpec((1, 1, gp), lambda l: (l, 0, 0))
    spec_b = pl.BlockSpec((1, S5_GROUPS, ph), lambda l: (l, 0, 0))
    spec_t = pl.BlockSpec((1, nt, SUBLANES, gp), lambda l: (l, 0, 0, 0))
    tre, tim, bbr, bbi = pl.pallas_call(
        _s5_prep_kernel,
        name="s5_prep",
        grid=(DEPTH,),
        in_specs=[spec_f, spec_f, spec_f, spec_b, spec_b, spec_b, spec_b, spec_b],
        out_specs=[spec_t, spec_t, spec_b, spec_b],
        out_shape=[
            jax.ShapeDtypeStruct((DEPTH, nt, SUBLANES, gp), f32),
            jax.ShapeDtypeStruct((DEPTH, nt, SUBLANES, gp), f32),
            jax.ShapeDtypeStruct((DEPTH, S5_GROUPS, ph), f32),
            jax.ShapeDtypeStruct((DEPTH, S5_GROUPS, ph), f32),
        ],
        compiler_params=_cparams(("parallel",)),
    )(flat(lam_re), flat(lam_im), flat(ls_f), rep(lam_re), rep(lam_im), ls_b,
      b_re.reshape(DEPTH, S5_GROUPS, ph), b_im.reshape(DEPTH, S5_GROUPS, ph))

    def interleave(re, im):
        lead = re.shape[:-1]
        both = jnp.stack([re.reshape(*lead, S5_NJ, S5_HALF), im.reshape(*lead, S5_NJ, S5_HALF)],
                         axis=-2)
        return both.reshape(*lead, S5_NJ * 2 * S5_HALF)

    tab = interleave(tre, tim)
    gl = LANES // S5_GROUP
    eye = jnp.eye(gl, dtype=f32)
    bb = jnp.stack([bbr, bbi], axis=1).reshape(DEPTH, 2, S5_NJ, gl, S5_STATE, S5_GROUP)
    bbd = jnp.einsum('lcjgph,gk->ljghckp', bb, eye)
    bbd = bbd.reshape(DEPTH, S5_NJ, LANES, 2 * S5_HALF).astype(bf16)
    return tab, bbd


def _s5_c_layout(c):
    gl = LANES // S5_GROUP
    eye = jnp.eye(gl, dtype=f32)
    c6 = c.reshape(DEPTH, S5_NJ, gl, S5_GROUP, S5_STATE)
    cbd = jnp.einsum('ljghp,gk->ljgpkh', c6, eye)
    return cbd.reshape(DEPTH, S5_NJ, S5_HALF, LANES).astype(bf16)


def _s5_segment_perm():
    pm = np.zeros((T_S5, T_S5), np.float32)
    for r in range(SUBLANES):
        for t in range(S5_SEG):
            pm[t * SUBLANES + r, r * S5_SEG + t] = 1.0
    return jnp.asarray(pm, bf16), jnp.asarray(pm.T, bf16)


def _lbs_kernel(x_ref, o_ref):
    rows = [x_ref[i:i + 1, :] for i in range(DEPTH)]
    m = functools.reduce(jnp.maximum, rows)
    es = [jnp.exp(r - m) for r in rows]
    tot = functools.reduce(lambda a, b: a + b, es)
    ps = [e / tot for e in es]
    acc = ps[0]
    out = [acc - ps[0]]
    for i in range(1, DEPTH):
        acc = acc + ps[i]
        out.append(acc - ps[0])
    o_ref[...] = jnp.concatenate(out, axis=0)


def _lbs(hg_lower_bounds):
    return pl.pallas_call(
        _lbs_kernel,
        name="hg_lower_bounds",
        out_shape=jax.ShapeDtypeStruct((DEPTH, W_MIX), f32),
    )(hg_lower_bounds)


def _s5_kernel(u_ref, ga_ref, pm_ref, pmt_ref, bbd_ref, cre_ref, cim_ref, tab_ref,
               d_ref, wglu_ref, bglu_ref, o_ref, st_ref, bu_ref, xb_ref, y_ref):
    hw = S5_HALF
    r8 = SUBLANES

    @pl.when(pl.program_id(1) == 0)
    def _():
        st_ref[...] = jnp.zeros_like(st_ref)

    pm = pm_ref[...]
    up = jnp.dot(pm, u_ref[...], preferred_element_type=f32)
    gap = jnp.dot(pm, ga_ref[...], preferred_element_type=f32)
    up_bf = up.astype(bf16)
    row8 = lax.broadcasted_iota(jnp.int32, (r8, hw), 0)

    for j in range(S5_NJ):
        slot = j
        lanes = slice(j * LANES, (j + 1) * LANES)
        tl = slice(j * 2 * hw, (j + 1) * 2 * hw)
        bu_ref[slot] = jnp.dot(up_bf[:, lanes], bbd_ref[0, j], preferred_element_type=f32)
        a1 = tab_ref[0, S5_KS_STEPS + 1, :, tl]
        are = a1[:, :hw]
        aim = a1[:, hw:]

        xre = bu_ref[slot, 0:r8, :hw]
        xim = bu_ref[slot, 0:r8, hw:]
        for t in range(1, S5_SEG):
            rows = slice(t * r8, (t + 1) * r8)
            xre, xim = (are * xre - aim * xim + bu_ref[slot, rows, :hw],
                        are * xim + aim * xre + bu_ref[slot, rows, hw:])
            bu_ref[slot, rows, :hw] = xre
            bu_ref[slot, rows, hw:] = xim

        for k in range(S5_KS_STEPS):
            mk = tab_ref[0, k, :, tl]
            mre = mk[:, :hw]
            mim = mk[:, hw:]
            sre = pltpu.roll(xre, 1 << k, 0)
            sim = pltpu.roll(xim, 1 << k, 0)
            xre, xim = xre + mre * sre - mim * sim, xim + mre * sim + mim * sre
        ak = tab_ref[0, S5_KS_STEPS, :, tl]
        cre = st_ref[j:j + 1, :hw]
        cim = st_ref[j:j + 1, hw:]
        fre = xre + ak[:, :hw] * cre - ak[:, hw:] * cim
        fim = xim + ak[:, :hw] * cim + ak[:, hw:] * cre
        st_ref[j:j + 1, :hw] = fre[r8 - 1:, :]
        st_ref[j:j + 1, hw:] = fim[r8 - 1:, :]
        ire = jnp.where(row8 == 0, cre, pltpu.roll(fre, 1, 0))
        iim = jnp.where(row8 == 0, cim, pltpu.roll(fim, 1, 0))

        wre, wim = ire, iim
        for t2 in range(S5_SEG // 2):
            rows = slice(t2 * BF16_ROWS, (t2 + 1) * BF16_ROWS)
            wre0, wim0 = are * wre - aim * wim, are * wim + aim * wre
            wre, wim = are * wre0 - aim * wim0, are * wim0 + aim * wre0
            xr = bu_ref[slot, rows, :hw] + jnp.concatenate([wre0, wre], axis=0)
            xi = bu_ref[slot, rows, hw:] + jnp.concatenate([wim0, wim], axis=0)
            xb_ref[slot, rows, :hw] = xr.astype(bf16)
            xb_ref[slot, rows, hw:] = xi.astype(bf16)

        y = (jnp.dot(xb_ref[slot, :, :hw], cre_ref[0, j], preferred_element_type=f32)
             - jnp.dot(xb_ref[slot, :, hw:], cim_ref[0, j], preferred_element_type=f32))
        y = y + d_ref[0, :, lanes] * up[:, lanes]
        y_ref[:, lanes] = jax.nn.gelu(y)

    y = y_ref[...]
    gl = jnp.dot(y.astype(bf16), wglu_ref[0], preferred_element_type=f32) + bglu_ref[0]
    outp = (y * _sigmoid(gl) * (gap * _sigmoid(gap))).astype(bf16)
    o_ref[...] = jnp.dot(pmt_ref[...], outp, preferred_element_type=f32).astype(bf16)


def _s5(z, l, bsz, seq, pm, pmt, bbd, cre, cim, tab, d, wglu, bglu):
    nc = seq // T_S5
    row = lambda b, c: b * nc + c
    lsel3 = lambda b, c: (l, 0, 0)
    lsel4 = lambda b, c: (l, 0, 0, 0)
    perm = pl.BlockSpec((T_S5, T_S5), lambda b, c: (0, 0))
    return pl.pallas_call(
        _s5_kernel,
        name="s5_mixer",
        grid=(bsz, nc),
        in_specs=[
            pl.BlockSpec((T_S5, W_MIX), lambda b, c: (row(b, c), 0)),
            pl.BlockSpec((T_S5, W_MIX), lambda b, c: (row(b, c), 1)),
            perm, perm,
            pl.BlockSpec((1, S5_NJ, LANES, 2 * S5_HALF), lsel4),
            pl.BlockSpec((1, S5_NJ, S5_HALF, LANES), lsel4),
            pl.BlockSpec((1, S5_NJ, S5_HALF, LANES), lsel4),
            pl.BlockSpec((1, S5_NTAB, SUBLANES, S5_NJ * 2 * S5_HALF), lsel4),
            pl.BlockSpec((1, 1, W_MIX), lsel3),
            pl.BlockSpec((1, W_MIX, W_MIX), lsel3),
            pl.BlockSpec((1, 1, W_MIX), lsel3),
        ],
        out_specs=pl.BlockSpec((T_S5, W_MIX), lambda b, c: (row(b, c), 0)),
        out_shape=jax.ShapeDtypeStruct((bsz * seq, W_MIX), bf16),
        scratch_shapes=[
            pltpu.VMEM((S5_NJ, 2 * S5_HALF), f32),
            pltpu.VMEM((S5_NJ, T_S5, 2 * S5_HALF), f32),
            pltpu.VMEM((S5_NJ, T_S5, 2 * S5_HALF), bf16),
            pltpu.VMEM((T_S5, W_MIX), f32),
        ],
        compiler_params=_cparams(("parallel", "arbitrary")),
    )(z, z, pm, pmt, bbd, cre, cim, tab, d, wglu, bglu)


def _rg_kernel(x_ref, g_ref, cw_ref, cb_ref, wa_ref, wx_ref, ba_ref, bx_ref, lam_ref,
               o_ref, tail_ref, h_ref, xe_ref, a_ref, b_ref):
    t = x_ref.shape[0]
    c = pl.program_id(1)

    @pl.when(c == 0)
    def _():
        tail_ref[...] = jnp.zeros_like(tail_ref)
        h_ref[...] = jnp.zeros_like(h_ref)

    x = x_ref[...].astype(f32)
    xe_ref[0:SUBLANES, :] = tail_ref[...]
    xe_ref[SUBLANES:SUBLANES + t, :] = x
    tail_ref[...] = x[t - SUBLANES:t, :]
    xc = cb_ref[0]
    for k in range(RG_CONV):
        off = SUBLANES - (RG_CONV - 1) + k
        xc = xc + cw_ref[0, k:k + 1, :] * xe_ref[off:off + t, :]

    rs, ins = [], []
    for m in range(RG_ND):
        sl = slice(m * RG_DIAG, (m + 1) * RG_DIAG)
        xs = xc[:, sl].astype(bf16)
        rs.append(jnp.dot(xs, wa_ref[0, m], preferred_element_type=f32))
        ins.append(jnp.dot(xs, wx_ref[0, m], preferred_element_type=f32))
    r = _sigmoid(jnp.concatenate(rs, axis=1) + ba_ref[0])
    i = _sigmoid(jnp.concatenate(ins, axis=1) + bx_ref[0])

    nlam = -lam_ref[0]
    softplus = jnp.maximum(nlam, 0.0) + jnp.log1p(jnp.exp(-jnp.abs(nlam)))
    log_a = (-RG_C * softplus) * r
    a = jnp.exp(log_a)
    mult = jnp.sqrt(-jnp.tanh(log_a) * (a * a + 1.0))
    row = lax.broadcasted_iota(jnp.int32, (t, W_MIX), 0)
    mult = jnp.where(jnp.logical_and(row == 0, c == 0), 1.0, mult)
    a_ref[...] = a
    b_ref[...] = mult * (i * xc)

    row8 = lax.broadcasted_iota(jnp.int32, (SUBLANES, W_MIX), 0)

    def body(rb, carry):
        row0 = pl.multiple_of(rb * SUBLANES, SUBLANES)
        av = a_ref[pl.ds(row0, SUBLANES), :]
        bv = b_ref[pl.ds(row0, SUBLANES), :]
        for k in range(3):
            s = 1 << k
            keep = row8 >= s
            sa = jnp.where(keep, pltpu.roll(av, s, 0), 1.0)
            sb = jnp.where(keep, pltpu.roll(bv, s, 0), 0.0)
            bv = bv + av * sb
            av = av * sa
        hv = bv + av * carry
        b_ref[pl.ds(row0, SUBLANES), :] = hv
        return hv[SUBLANES - 1:, :]

    carry = lax.fori_loop(0, t // SUBLANES, body, h_ref[...])
    h_ref[...] = carry
    g = g_ref[...].astype(f32)
    o_ref[...] = (b_ref[...] * (g * _sigmoid(g))).astype(bf16)


def _rg(z, l, bsz, seq, cw, cb, wa_bd, wx_bd, ba, bx, lam):
    nc = seq // T_RG
    row = lambda b, c: b * nc + c
    lsel3 = lambda b, c: (l, 0, 0)
    lsel4 = lambda b, c: (l, 0, 0, 0)
    vec = pl.BlockSpec((1, 1, W_MIX), lsel3)
    return pl.pallas_call(
        _rg_kernel,
        name="rglru_mixer",
        grid=(bsz, nc),
        in_specs=[
            pl.BlockSpec((T_RG, W_MIX), lambda b, c: (row(b, c), 2)),
            pl.BlockSpec((T_RG, W_MIX), lambda b, c: (row(b, c), 3)),
            pl.BlockSpec((1, RG_CONV, W_MIX), lsel3),
            vec,
            pl.BlockSpec((1, RG_ND, RG_DIAG, RG_DIAG), lsel4),
            pl.BlockSpec((1, RG_ND, RG_DIAG, RG_DIAG), lsel4),
            vec, vec, vec,
        ],
        out_specs=pl.BlockSpec((T_RG, W_MIX), lambda b, c: (row(b, c), 0)),
        out_shape=jax.ShapeDtypeStruct((bsz * seq, W_MIX), bf16),
        scratch_shapes=[
            pltpu.VMEM((SUBLANES, W_MIX), f32),
            pltpu.VMEM((1, W_MIX), f32),
            pltpu.VMEM((T_RG + SUBLANES, W_MIX), f32),
            pltpu.VMEM((T_RG, W_MIX), f32),
            pltpu.VMEM((T_RG, W_MIX), f32),
        ],
        compiler_params=_cparams(("parallel", "arbitrary")),
    )(z, z, cw, cb, wa_bd, wx_bd, ba, bx, lam)


def _rg_blockdiag(w):
    per = RG_DIAG // RG_BLOCK
    eye = jnp.eye(per, dtype=f32)
    w5 = w.reshape(DEPTH, RG_ND, per, RG_BLOCK, RG_BLOCK)
    bd = jnp.einsum('lmnij,nk->lmnikj', w5, eye)
    return bd.reshape(DEPTH, RG_ND, RG_DIAG, RG_DIAG).astype(bf16)


def _hg_level_masks():
    t = np.arange(T_HG)[:, None]
    s = np.arange(T_HG)[None, :]
    masks = [((t // (2 * m)) == (s // (2 * m))) & ((t & m) != 0) & ((s & m) == 0)
             for m in HG_LEVELS]
    return jnp.asarray(np.stack(masks).astype(np.float32))


def _hg_kernel(q_ref, f_ref, i_ref, g_ref, lm_ref, lb_ref, nw_ref, o_ref,
               st_ref, q_s, k_s, gc_s, r_s, att_s):
    t = q_ref.shape[0]
    w = q_ref.shape[1]
    dk = HG_DK
    nb = t // SUBLANES

    @pl.when(pl.program_id(1) == 0)
    def _():
        st_ref[...] = jnp.zeros_like(st_ref)

    row = lax.broadcasted_iota(jnp.int32, (t, t), 0)
    col = lax.broadcasted_iota(jnp.int32, (t, t), 1)
    tri = (row >= col).astype(f32)
    roww = lax.broadcasted_iota(jnp.int32, (t, w), 0)
    nt_dims = (((1,), (1,)), ((), ()))
    tn_dims = (((0,), (0,)), ((), ()))

    qr = q_ref[...].astype(f32)
    lb = lb_ref[0]
    fg = lb + (1.0 - lb) * _sigmoid(f_ref[...].astype(f32))
    q_s[...] = qr * _sigmoid(qr)
    k_s[...] = 1.0 - fg
    gc_s[...] = jnp.dot(tri, jnp.log(fg), preferred_element_type=f32,
                        precision=lax.Precision.HIGHEST)

    for li, m in enumerate(HG_LEVELS):
        if m >= SUBLANES:
            pieces = []
            for p in range(t // (2 * m)):
                lo = slice(2 * p * m, 2 * p * m + m)
                up = slice(2 * p * m + m, 2 * p * m + 2 * m)
                b = gc_s[2 * p * m + m - 1:2 * p * m + m, :]
                pieces.append(k_s[lo, :] * jnp.exp(b - gc_s[lo, :]))
                pieces.append(q_s[up, :] * jnp.exp(gc_s[up, :] - b))
            r = jnp.concatenate(pieces, axis=0)
        else:
            gc = gc_s[...]
            g3 = gc.reshape(nb, SUBLANES, w)

            def bcast_row(i, g3=g3):
                return jnp.broadcast_to(g3[:, i:i + 1, :], g3.shape).reshape(t, w)

            if m == 4:
                bnd = bcast_row(3)
            elif m == 2:
                bnd = jnp.where((roww & 4) == 0, bcast_row(1), bcast_row(5))
            else:
                bnd = jnp.where((roww & 1) == 0, gc, pltpu.roll(gc, 1, 0))
            upper = (roww & m) != 0
            d = gc - bnd
            e = jnp.exp(jnp.where(upper, d, -d))
            r = jnp.where(upper, q_s[...], k_s[...]) * e
        r_s[...] = r.astype(bf16)
        for h in range(HG_HEADS):
            rh = r_s[:, h * dk:(h + 1) * dk]
            al = lm_ref[li] * lax.dot_general(rh, rh, nt_dims, preferred_element_type=f32)
            if li == 0:
                att_s[h] = al
            else:
                att_s[h] += al

    for h in range(HG_HEADS):
        sl = slice(h * dk, (h + 1) * dk)
        q = q_s[:, sl]
        k = k_s[:, sl]
        gc = gc_s[:, sl]
        vb = i_ref[:, sl]
        st = st_ref[h]
        intra = jnp.dot(att_s[h].astype(bf16), vb, preferred_element_type=f32)
        intra = intra + jnp.sum(q * k, axis=1, keepdims=True) * vb.astype(f32)
        qg = (q * jnp.exp(gc)).astype(bf16)
        inter = lax.dot_general(qg, st.astype(bf16), nt_dims, preferred_element_type=f32)
        glast = gc[t - 1:t, :]
        khat = (k * jnp.exp(glast - gc)).astype(bf16)
        st_ref[h] = jnp.exp(glast) * st + lax.dot_general(
            vb, khat, tn_dims, preferred_element_type=f32)

        o = intra + inter
        o = o * lax.rsqrt(jnp.mean(o * o, axis=-1, keepdims=True) + EPS) * nw_ref[0, :, sl]
        gate = g_ref[:, sl].astype(f32)
        o_ref[:, sl] = (o * (gate * _sigmoid(gate))).astype(bf16)


def _hg(z, l, bsz, seq, masks, lbs, nw):
    nc = seq // T_HG
    row = lambda b, c: b * nc + c
    lsel3 = lambda b, c: (l, 0, 0)
    zspec = lambda col: pl.BlockSpec((T_HG, W_MIX), lambda b, c: (row(b, c), col))
    vec = pl.BlockSpec((1, 1, W_MIX), lsel3)
    return pl.pallas_call(
        _hg_kernel,
        name="hgrn2_mixer",
        grid=(bsz, nc),
        in_specs=[zspec(4), zspec(5), zspec(6), zspec(7),
                  pl.BlockSpec((len(HG_LEVELS), T_HG, T_HG), lambda b, c: (0, 0, 0)),
                  vec, vec],
        out_specs=pl.BlockSpec((T_HG, W_MIX), lambda b, c: (row(b, c), 0)),
        out_shape=jax.ShapeDtypeStruct((bsz * seq, W_MIX), bf16),
        scratch_shapes=[
            pltpu.VMEM((HG_HEADS, HG_DK, HG_DK), f32),
            pltpu.VMEM((T_HG, W_MIX), f32),
            pltpu.VMEM((T_HG, W_MIX), f32),
            pltpu.VMEM((T_HG, W_MIX), f32),
            pltpu.VMEM((T_HG, W_MIX), bf16),
            pltpu.VMEM((HG_HEADS, T_HG, T_HG), f32),
        ],
        compiler_params=_cparams(("parallel", "arbitrary")),
    )(z, z, z, z, masks, lbs, nw)


def _merge_kernel(last, ya_ref, yb_ref, yc_ref, g0_ref, g1_ref, g2_ref, x_ref, wb_ref, wo_ref,
                  nw_ref, *out_refs):
    merged = None
    for n, (y_ref, gl_ref) in enumerate(((ya_ref, g0_ref), (yb_ref, g1_ref), (yc_ref, g2_ref))):
        br = jnp.dot(y_ref[...], wb_ref[0, n], preferred_element_type=f32)
        term = _sigmoid(gl_ref[...].astype(f32)) * br
        merged = term if merged is None else merged + term
    xn = x_ref[...] + jnp.dot(merged.astype(bf16), wo_ref[0], preferred_element_type=f32)
    hn = _rms(xn, nw_ref[...])
    if last:
        out_refs[0][...] = hn
    else:
        out_refs[0][...] = xn
        out_refs[1][...] = hn.astype(bf16)


def _merge(ya, yb, yc, z, x2, l, wb, wo, nw, last, tm=256):
    m = x2.shape[0]
    yspec = pl.BlockSpec((tm, W_MIX), lambda i: (i, 0))
    gspec = lambda n: pl.BlockSpec((tm, D_MODEL), lambda i: (i, 4 + n))
    xspec = pl.BlockSpec((tm, D_MODEL), lambda i: (i, 0))
    if last:
        out_specs = [xspec]
        out_shape = [jax.ShapeDtypeStruct((m, D_MODEL), f32)]
    else:
        out_specs = [xspec, xspec]
        out_shape = [jax.ShapeDtypeStruct((m, D_MODEL), f32),
                     jax.ShapeDtypeStruct((m, D_MODEL), bf16)]
    return pl.pallas_call(
        functools.partial(_merge_kernel, last),
        name="merge_out",
        grid=(m // tm,),
        in_specs=[
            yspec, yspec, yspec, gspec(0), gspec(1), gspec(2), xspec,
            pl.BlockSpec((1, N_BRANCH, W_MIX, D_MODEL), lambda i: (l, 0, 0, 0),
                         pipeline_mode=pl.Buffered(1)),
            pl.BlockSpec((1, D_MODEL, D_MODEL), lambda i: (l, 0, 0),
                         pipeline_mode=pl.Buffered(1)),
            pl.BlockSpec((1, D_MODEL), lambda i: (0, 0)),
        ],
        out_specs=out_specs,
        out_shape=out_shape,
        compiler_params=_cparams(("parallel",)),
    )(ya, yb, yc, z, z, z, x2, wb, wo, nw)


def kernel(x, norm_w, w_in, s5_lambda_re, s5_lambda_im, s5_log_step, s5_b_re, s5_b_im, s5_c_re, s5_c_im, s5_d, s5_w_glu, s5_b_glu, rg_conv_w, rg_conv_b, rg_w_a, rg_b_a, rg_w_x, rg_b_x, rg_lambda, hg_lower_bounds, hg_norm_w, w_branch, w_out, final_norm_w):
    bsz, seq, _ = x.shape
    assert seq % T_S5 == 0 and seq % T_RG == 0 and seq % T_HG == 0
    x2 = x.reshape(bsz * seq, D_MODEL)

    tab, bbd = _s5_prep(s5_lambda_re, s5_lambda_im, s5_log_step, s5_b_re, s5_b_im)
    cre = _s5_c_layout(s5_c_re)
    cim = _s5_c_layout(s5_c_im)
    pm, pmt = _s5_segment_perm()
    wglu_bf = s5_w_glu.astype(bf16)
    wa_bd = _rg_blockdiag(rg_w_a)
    wx_bd = _rg_blockdiag(rg_w_x)
    lbs = _lbs(hg_lower_bounds.astype(f32))
    masks = _hg_level_masks()
    wb_bf = w_branch.astype(bf16)
    wo_bf = w_out.astype(bf16)
    vec3 = lambda a: a.reshape(DEPTH, 1, -1)

    h = _norm_bf16(x2, norm_w[0].reshape(1, D_MODEL))
    for l in range(DEPTH):
        last = l == DEPTH - 1
        z = _inproj(h, w_in, l)
        ya = _s5(z, l, bsz, seq, pm, pmt, bbd, cre, cim, tab, vec3(s5_d), wglu_bf,
                 vec3(s5_b_glu))
        yb = _rg(z, l, bsz, seq, rg_conv_w, vec3(rg_conv_b), wa_bd, wx_bd,
                 vec3(rg_b_a), vec3(rg_b_x), vec3(rg_lambda))
        yc = _hg(z, l, bsz, seq, masks, vec3(lbs), vec3(hg_norm_w))
        nw = (final_norm_w if last else norm_w[l + 1]).reshape(1, D_MODEL)
        outs = _merge(ya, yb, yc, z, x2, l, wb_bf, wo_bf, nw, last)
        if last:
            out = outs[0]
        else:
            x2, h = outs

    return out.reshape(bsz, seq, D_MODEL)
```
